```python
import math, functools
import jax, jax.numpy as jnp
from jax import lax
import numpy as np

D_MODEL = 1024
BATCH = 16
SEQ = 2048
DEPTH = 4
DEC_BATCH = 128
DEC_SEQ = 8
PAST_LEN = 8192
PAGE_SIZE = 128

HEAD_DIM = 64
FOX_HEADS = 6
FOX_KV_HEADS = 2
FOX_GROUP = FOX_HEADS // FOX_KV_HEADS
SB_HEADS = 6
SB_KV_HEADS = 2
SB_GROUP = SB_HEADS // SB_KV_HEADS
MLA_HEADS = 4
MLA_Q_RANK = 192
MLA_KV_RANK = 128
MLA_NOPE = 64
MLA_ROPE = 32
MLA_V = 64
ROPE_BASE = 10000.0
MEM_TOKENS = 256
MEM_HEADS = 4
MEM_HEAD_DIM = 64
D_FF = 4 * D_MODEL
Q_BLOCK = 128
EPS = 1e-6
FOX_BIAS_INIT = 3.0
FOX_W = FOX_HEADS * HEAD_DIM
MLA_W = MLA_HEADS * MLA_V
SB_W = SB_HEADS * HEAD_DIM
MIX_W = FOX_W + MLA_W + SB_W
ATT_SCALE = 1.0 / math.sqrt(HEAD_DIM)
MLA_SCALE = 1.0 / math.sqrt(MLA_NOPE + MLA_ROPE)
MEM_SCALE = 1.0 / math.sqrt(MEM_HEAD_DIM)
IN_SIZES = (FOX_HEADS * HEAD_DIM, FOX_KV_HEADS * HEAD_DIM, FOX_KV_HEADS * HEAD_DIM, FOX_HEADS,
            MLA_Q_RANK, MLA_KV_RANK, MLA_ROPE,
            SB_HEADS * HEAD_DIM, SB_KV_HEADS * HEAD_DIM, SB_KV_HEADS * HEAD_DIM)
N_IN = sum(IN_SIZES)

kernel_name = "hymba_fox_mla_stickbreak_decoder_step"


def _split_points():
    pts, acc = [], 0
    for s in IN_SIZES[:-1]:
        acc += s
        pts.append(acc)
    return pts


def rms_norm(x, g):
    xf = x.astype(jnp.float32)
    y = xf * lax.rsqrt(jnp.mean(xf * xf, axis=-1, keepdims=True) + EPS)
    return (y * g.astype(jnp.float32)).astype(x.dtype)


def rope(x, pos):
    half = MLA_ROPE // 2
    inv_freq = jnp.power(ROPE_BASE, -jnp.arange(half, dtype=jnp.float32) / half)
    ang = pos.astype(jnp.float32)[:, None] * inv_freq[None, :]
    shape = (1, pos.shape[0]) + (1,) * (x.ndim - 3) + (half,)
    cos = jnp.cos(ang).reshape(shape)
    sin = jnp.sin(ang).reshape(shape)
    xf = x.astype(jnp.float32)
    x1, x2 = xf[..., :half], xf[..., half:]
    return jnp.concatenate([x1 * cos - x2 * sin, x2 * cos + x1 * sin], axis=-1).astype(x.dtype)


def mixer_inputs(xn, pos, w_in, b_f, g_q, w_uq, g_kv, w_uk):
    B, T, _ = xn.shape
    parts = jnp.split(xn @ w_in, _split_points(), axis=-1)
    fq, fk, fv, ff, mqa, mkv, mkpe, sq, sk, sv = parts
    fox_q = fq.reshape(B, T, FOX_KV_HEADS, FOX_GROUP, HEAD_DIM)
    fox_k = fk.reshape(B, T, FOX_KV_HEADS, HEAD_DIM)
    fox_v = fv.reshape(B, T, FOX_KV_HEADS, HEAD_DIM)
    fox_logf = jax.nn.log_sigmoid((ff + b_f).astype(jnp.float32)).astype(xn.dtype)
    q = (rms_norm(mqa, g_q) @ w_uq).reshape(B, T, MLA_HEADS, MLA_NOPE + MLA_ROPE)
    q_nope = q[..., :MLA_NOPE]
    q_pe = rope(q[..., MLA_NOPE:], pos)
    q_lat = jnp.einsum('bthn,chn->bthc', q_nope, w_uk)
    ckv = rms_norm(mkv, g_kv)
    kpe = rope(mkpe, pos)
    sb_q = sq.reshape(B, T, SB_KV_HEADS, SB_GROUP, HEAD_DIM)
    sb_k = sk.reshape(B, T, SB_KV_HEADS, HEAD_DIM)
    sb_v = sv.reshape(B, T, SB_KV_HEADS, HEAD_DIM)
    queries = (fox_q, q_lat, q_pe, sb_q)
    rows = (fox_k, fox_v, fox_logf, ckv, kpe, sb_k, sb_v)
    return queries, rows


def attend_block(fox_q, q_lat, q_pe, sb_q, fox_cq, q_pos, keys, fox_ck, k_pos, w_uv):
    fox_k, fox_v, ckv, kpe, sb_k, sb_v = keys
    B, Tq = fox_q.shape[:2]
    Tk = fox_k.shape[1]
    f32 = jnp.float32
    causal = k_pos[None, :] <= q_pos[:, None]
    strict = k_pos[None, :] < q_pos[:, None]
    s = jnp.einsum('bqhgd,bkhd->bhgqk', fox_q, fox_k).astype(f32) * ATT_SCALE
    cq = fox_cq.reshape(B, Tq, FOX_KV_HEADS, FOX_GROUP).transpose(0, 2, 3, 1)
    ck = fox_ck.reshape(B, Tk, FOX_KV_HEADS, FOX_GROUP).transpose(0, 2, 3, 1)
    s = jnp.where(causal, s + cq[..., :, None] - ck[..., None, :], -jnp.inf)
    p = jax.nn.softmax(s, axis=-1)
    o_fox = jnp.einsum('bhgqk,bkhd->bqhgd', p.astype(fox_v.dtype), fox_v).reshape(B, Tq, FOX_W)
    s = (jnp.einsum('bqhc,bkc->bhqk', q_lat, ckv).astype(f32)
         + jnp.einsum('bqhr,bkr->bhqk', q_pe, kpe).astype(f32)) * MLA_SCALE
    p = jax.nn.softmax(jnp.where(causal, s, -jnp.inf), axis=-1)
    o_lat = jnp.einsum('bhqk,bkc->bqhc', p.astype(ckv.dtype), ckv)
    o_mla = jnp.einsum('bqhc,chv->bqhv', o_lat, w_uv).reshape(B, Tq, MLA_W)
    z = jnp.einsum('bqhgd,bkhd->bhgqk', sb_q, sb_k).astype(f32) * ATT_SCALE
    log_keep = jnp.where(strict, jax.nn.log_sigmoid(-z), 0.0)
    suffix = lax.cumsum(log_keep, axis=4, reverse=True) - log_keep
    a = jnp.where(strict, jnp.exp(jax.nn.log_sigmoid(z) + suffix), 0.0)
    o_sb = jnp.einsum('bhgqk,bkhd->bqhgd', a.astype(sb_v.dtype), sb_v).reshape(B, Tq, SB_W)
    return o_fox, o_mla, o_sb


def prompt_attention(queries, rows, w_uv, pos):
    fox_q, q_lat, q_pe, sb_q = queries
    fox_k, fox_v, fox_logf, ckv, kpe, sb_k, sb_v = rows
    fox_c = lax.cumsum(fox_logf.astype(jnp.float32), axis=1)
    keys = (fox_k, fox_v, ckv, kpe, sb_k, sb_v)
    B, T = fox_q.shape[:2]
    nb = T // Q_BLOCK

    def to_blocks(a):
        return jnp.moveaxis(a.reshape((B, nb, Q_BLOCK) + a.shape[2:]), 1, 0)

    def one_block(xs):
        fq, ql, qp, sq, cq, qpos = xs
        return attend_block(fq, ql, qp, sq, cq, qpos, keys, fox_c, pos, w_uv)

    xs = tuple(to_blocks(a) for a in (fox_q, q_lat, q_pe, sb_q, fox_c)) + (pos.reshape(nb, Q_BLOCK),)
    outs = lax.map(one_block, xs)
    return tuple(jnp.moveaxis(o, 0, 1).reshape((B, T) + o.shape[3:]) for o in outs)


def sample_attention(queries, rows, w_uv, pos, past):
    full = tuple(jnp.concatenate([p, r], axis=1) for p, r in zip(past, rows))
    fox_k, fox_v, fox_logf, ckv, kpe, sb_k, sb_v = full
    fox_c = lax.cumsum(fox_logf.astype(jnp.float32), axis=1)
    n_past = past[0].shape[1]
    k_pos = jnp.arange(fox_k.shape[1], dtype=jnp.int32)
    fox_q, q_lat, q_pe, sb_q = queries
    return attend_block(fox_q, q_lat, q_pe, sb_q, fox_c[:, n_past:], pos,
                        (fox_k, fox_v, ckv, kpe, sb_k, sb_v), fox_c, k_pos, w_uv)


def mem_attend(hn, mem_k, mem_v, w_q, w_o):
    B, T, _ = hn.shape
    q = (hn @ w_q).reshape(B, T, MEM_HEADS, MEM_HEAD_DIM)
    s = jnp.einsum('bqhd,bkhd->bhqk', q, mem_k).astype(jnp.float32) * MEM_SCALE
    p = jax.nn.softmax(s, axis=-1)
    o = jnp.einsum('bhqk,bkhd->bqhd', p.astype(mem_v.dtype), mem_v).reshape(B, T, MEM_HEADS * MEM_HEAD_DIM)
    return o @ w_o


def layer(h, pos, mem_k, mem_v, lp, attention):
    (g_mix, w_in, b_f, g_q, w_uq, g_kv, w_uk, w_uv, g_of, g_om, g_os, w_out,
     g_mem, w_mq, w_mo, g_mlp, w_up, w_down) = lp
    xn = rms_norm(h, g_mix)
    queries, rows = mixer_inputs(xn, pos, w_in, b_f, g_q, w_uq, g_kv, w_uk)
    o_fox, o_mla, o_sb = attention(queries, rows, w_uv)
    o = jnp.concatenate([rms_norm(o_fox, g_of), rms_norm(o_mla, g_om), rms_norm(o_sb, g_os)], axis=-1)
    h = h + o @ w_out
    h = h + mem_attend(rms_norm(h, g_mem), mem_k, mem_v, w_mq, w_mo)
    hn = rms_norm(h, g_mlp)
    h = h + jnp.square(jax.nn.relu(hn @ w_up)) @ w_down
    return h, rows


def setup_inputs(seed: int = 0) -> dict:
    key = jax.random.key(seed)
    counter = [0]

    def nk():
        counter[0] += 1
        return jax.random.fold_in(key, counter[0])

    f32 = jnp.float32

    def nrm(shape, scale=1.0):
        return jax.random.normal(nk(), shape, f32) * scale

    def gain(shape):
        return 1.0 + 0.02 * jax.random.normal(nk(), shape, f32)

    n_pages = PAST_LEN // PAGE_SIZE
    used = DEC_BATCH * n_pages
    pool = used + used // 4
    pg = (DEPTH, pool, PAGE_SIZE)
    return {
        'x_prompt': nrm((BATCH, SEQ, D_MODEL)),
        'x_sample': nrm((DEC_BATCH, DEC_SEQ, D_MODEL)),
        'cache_fox_k': nrm(pg + (FOX_KV_HEADS, HEAD_DIM)),
        'cache_fox_v': nrm(pg + (FOX_KV_HEADS, HEAD_DIM)),
        'cache_fox_logf': jax.nn.log_sigmoid(FOX_BIAS_INIT + nrm(pg + (FOX_HEADS,))),
        'cache_mla_ckv': nrm(pg + (MLA_KV_RANK,)),
        'cache_mla_kpe': nrm(pg + (MLA_ROPE,)),
        'cache_sb_k': nrm(pg + (SB_KV_HEADS, HEAD_DIM)),
        'cache_sb_v': nrm(pg + (SB_KV_HEADS, HEAD_DIM)),
        'cache_mem_k': nrm((DEPTH, DEC_BATCH, MEM_TOKENS, MEM_HEADS, MEM_HEAD_DIM)),
        'cache_mem_v': nrm((DEPTH, DEC_BATCH, MEM_TOKENS, MEM_HEADS, MEM_HEAD_DIM)),
        'page_table': jax.random.permutation(nk(), pool)[:used].reshape(DEC_BATCH, n_pages).astype(jnp.int32),
        'mem_prompt': nrm((BATCH, MEM_TOKENS, D_MODEL)),
        'g_mix': gain((DEPTH, D_MODEL)),
        'w_in': nrm((DEPTH, D_MODEL, N_IN), D_MODEL ** -0.5),
        'b_fox_f': FOX_BIAS_INIT + nrm((DEPTH, FOX_HEADS), 0.1),
        'g_mla_q': gain((DEPTH, MLA_Q_RANK)),
        'w_mla_uq': nrm((DEPTH, MLA_Q_RANK, MLA_HEADS * (MLA_NOPE + MLA_ROPE)), MLA_Q_RANK ** -0.5),
        'g_mla_kv': gain((DEPTH, MLA_KV_RANK)),
        'w_mla_uk': nrm((DEPTH, MLA_KV_RANK, MLA_HEADS, MLA_NOPE), MLA_KV_RANK ** -0.5),
        'w_mla_uv': nrm((DEPTH, MLA_KV_RANK, MLA_HEADS, MLA_V), MLA_KV_RANK ** -0.5),
        'g_out_fox': gain((DEPTH, FOX_W)),
        'g_out_mla': gain((DEPTH, MLA_W)),
        'g_out_sb': gain((DEPTH, SB_W)),
        'w_out': nrm((DEPTH, MIX_W, D_MODEL), MIX_W ** -0.5),
        'g_mem': gain((DEPTH, D_MODEL)),
        'w_mem_q': nrm((DEPTH, D_MODEL, MEM_HEADS * MEM_HEAD_DIM), D_MODEL ** -0.5),
        'w_mem_k': nrm((DEPTH, D_MODEL, MEM_HEADS * MEM_HEAD_DIM), D_MODEL ** -0.5),
        'w_mem_v': nrm((DEPTH, D_MODEL, MEM_HEADS * MEM_HEAD_DIM), D_MODEL ** -0.5),
        'w_mem_o': nrm((DEPTH, MEM_HEADS * MEM_HEAD_DIM, D_MODEL), (MEM_HEADS * MEM_HEAD_DIM) ** -0.5),
        'g_mlp': gain((DEPTH, D_MODEL)),
        'w_up': nrm((DEPTH, D_MODEL, D_FF), D_MODEL ** -0.5),
        'w_down': nrm((DEPTH, D_FF, D_MODEL), D_FF ** -0.5),
        'g_final': gain((D_MODEL,)),
    }


def reference(x_prompt, x_sample, cache_fox_k, cache_fox_v, cache_fox_logf, cache_mla_ckv, cache_mla_kpe,
              cache_sb_k, cache_sb_v, cache_mem_k, cache_mem_v, page_table, mem_prompt,
              g_mix, w_in, b_fox_f, g_mla_q, w_mla_uq, g_mla_kv, w_mla_uk, w_mla_uv,
              g_out_fox, g_out_mla, g_out_sb, w_out, g_mem, w_mem_q, w_mem_k, w_mem_v, w_mem_o,
              g_mlp, w_up, w_down, g_final):
    n_prompt, seq = x_prompt.shape[:2]
    n_dec, n_new = x_sample.shape[:2]
    pos_p = jnp.arange(seq, dtype=jnp.int32)
    pos_s = PAST_LEN + jnp.arange(n_new, dtype=jnp.int32)
    paged = (cache_fox_k, cache_fox_v, cache_fox_logf, cache_mla_ckv, cache_mla_kpe, cache_sb_k, cache_sb_v)
    hp, hs = x_prompt, x_sample
    prompt_rows, sample_rows, mem_ks, mem_vs = [], [], [], []
    for l in range(DEPTH):
        lp = (g_mix[l], w_in[l], b_fox_f[l], g_mla_q[l], w_mla_uq[l], g_mla_kv[l], w_mla_uk[l], w_mla_uv[l],
              g_out_fox[l], g_out_mla[l], g_out_sb[l], w_out[l], g_mem[l], w_mem_q[l], w_mem_o[l],
              g_mlp[l], w_up[l], w_down[l])
        mk = (mem_prompt @ w_mem_k[l]).reshape(n_prompt, MEM_TOKENS, MEM_HEADS, MEM_HEAD_DIM)
        mv = (mem_prompt @ w_mem_v[l]).reshape(n_prompt, MEM_TOKENS, MEM_HEADS, MEM_HEAD_DIM)
        hp, rows_p = layer(hp, pos_p, mk, mv, lp, functools.partial(prompt_attention, pos=pos_p))
        past = tuple(c[l, page_table].reshape((n_dec, PAST_LEN) + c.shape[3:]) for c in paged)
        hs, rows_s = layer(hs, pos_s, cache_mem_k[l], cache_mem_v[l], lp,
                           functools.partial(sample_attention, pos=pos_s, past=past))
        prompt_rows.append(rows_p)
        sample_rows.append(rows_s)
        mem_ks.append(mk)
        mem_vs.append(mv)
    y_prompt = rms_norm(hp, g_final)
    y_sample = rms_norm(hs, g_final)
    p_fox_k, p_fox_v, p_fox_logf, p_mla_ckv, p_mla_kpe, p_sb_k, p_sb_v = (jnp.stack(a) for a in zip(*prompt_rows))
    s_fox_k, s_fox_v, s_fox_logf, s_mla_ckv, s_mla_kpe, s_sb_k, s_sb_v = (jnp.stack(a) for a in zip(*sample_rows))
    p_mem_k = jnp.stack(mem_ks)
    p_mem_v = jnp.stack(mem_vs)
    return (y_prompt, y_sample,
            p_fox_k, p_fox_v, p_fox_logf, p_mla_ckv, p_mla_kpe, p_sb_k, p_sb_v, p_mem_k, p_mem_v,
            s_fox_k, s_fox_v, s_fox_logf, s_mla_ckv, s_mla_kpe, s_sb_k, s_sb_v)
```

```python
import functools
import math

import jax
import jax.numpy as jnp
from jax import lax
from jax.experimental import pallas as pl
from jax.experimental.pallas import tpu as pltpu

F32 = jnp.float32
BF16 = jnp.bfloat16

D_MODEL = 1024
HEAD_DIM = 64
FOX_HEADS = 6
FOX_KV_HEADS = 2
SB_HEADS = 6
SB_KV_HEADS = 2
GROUP = 3
MLA_HEADS = 4
MLA_Q_RANK = 192
MLA_KV_RANK = 128
MLA_NOPE = 64
MLA_ROPE = 32
MLA_V = 64
ROPE_BASE = 10000.0
MEM_HEADS = 4
MEM_HEAD_DIM = 64
MEM_W = MEM_HEADS * MEM_HEAD_DIM
D_FF = 4 * D_MODEL
EPS = 1e-6
PAGE = 128
FOX_W = FOX_HEADS * HEAD_DIM
MLA_W = MLA_HEADS * MLA_V
SB_W = SB_HEADS * HEAD_DIM
KV_W = FOX_KV_HEADS * HEAD_DIM
ATT_SCALE = 1.0 / math.sqrt(HEAD_DIM)
MLA_SCALE = 1.0 / math.sqrt(MLA_NOPE + MLA_ROPE)
MEM_SCALE = 1.0 / math.sqrt(MEM_HEAD_DIM)

LANE = 128
VMEM_LIMIT = 48 * 1024 * 1024

C_FQ = 0
C_FK = C_FQ + FOX_W
C_FV = C_FK + KV_W
C_SQ = C_FV + KV_W
C_SK = C_SQ + SB_W
C_SV = C_SK + KV_W
C_CKV = C_SV + KV_W
C_MQA = C_CKV + MLA_KV_RANK
C_KPE = C_MQA + 2 * LANE
C_FF = C_KPE + LANE
N_INR = C_FF + LANE

NEG_INF = float("-inf")


def _cparams(sem):
    return pltpu.CompilerParams(dimension_semantics=sem, vmem_limit_bytes=VMEM_LIMIT)


def _log_sigmoid(x):
    return jnp.minimum(x, 0.0) - jnp.log(1.0 + jnp.exp(-jnp.abs(x)))


def _split3(x):
    hi = x.astype(BF16)
    r = x - hi.astype(F32)
    mid = r.astype(BF16)
    lo = (r - mid.astype(F32)).astype(BF16)
    return hi, mid, lo


def _split2(x):
    hi = x.astype(BF16)
    lo = (x - hi.astype(F32)).astype(BF16)
    return hi, lo


def _dot(a, b):
    return jnp.dot(a, b, preferred_element_type=F32)


def _dot_nt(a, b):
    return lax.dot_general(a, b, (((1,), (1,)), ((), ())), preferred_element_type=F32)


def _dot_tn(a, b):
    return lax.dot_general(a, b, (((0,), (0,)), ((), ())), preferred_element_type=F32)


def _rms(x, g, n=None):
    n = x.shape[-1] if n is None else n
    ms = jnp.sum(x * x, axis=-1, keepdims=True) * (1.0 / n)
    return x * lax.rsqrt(ms + EPS) * g


def _in_kernel(seg, tm, h_ref, g_ref, w_ref, wfft_ref, bf_ref, bft_ref, gq_ref, wuq_ref, gkv_ref,
               wuk_ref, cos_ref, sin_ref,
               fk32, fv32, sk32, sv32, ckv32, kpe32, logf_o,
               fq_b, fk_b, fv_b, sq_b, sk_b, sv_b, qlat_b, qpe_b, ckv_b, kpe_b, c_o, ct_o,
               carry_ref, carryt_ref):
    i = pl.program_id(0)
    x = h_ref[...]
    xn = _rms(x, g_ref[...]).astype(BF16)
    proj = _dot(xn, w_ref[...])

    fk = proj[:, C_FK:C_FK + KV_W]
    fv = proj[:, C_FV:C_FV + KV_W]
    sk = proj[:, C_SK:C_SK + KV_W]
    sv = proj[:, C_SV:C_SV + KV_W]
    fk32[...] = fk
    fv32[...] = fv
    sk32[...] = sk
    sv32[...] = sv
    fk_b[...] = fk.astype(BF16)
    fv_b[...] = fv.astype(BF16)
    sk_b[...] = sk.astype(BF16)
    sv_b[...] = sv.astype(BF16)
    fq_b[...] = (proj[:, C_FQ:C_FQ + FOX_W] * ATT_SCALE).astype(BF16)
    sq_b[...] = (proj[:, C_SQ:C_SQ + SB_W] * ATT_SCALE).astype(BF16)

    ckv = _rms(proj[:, C_CKV:C_CKV + MLA_KV_RANK], gkv_ref[...])
    ckv32[...] = ckv
    ckv_b[...] = ckv.astype(BF16)
    cos = cos_ref[...]
    sin = sin_ref[...]
    kblk = proj[:, C_KPE:C_KPE + LANE]
    kpe = kblk[:, :MLA_ROPE] * cos[:, :MLA_ROPE] + kblk[:, MLA_ROPE:2 * MLA_ROPE] * sin[:, :MLA_ROPE]
    kpe32[...] = kpe
    kpe_b[...] = kpe.astype(BF16)
    qa = _rms(proj[:, C_MQA:C_MQA + 2 * LANE], gq_ref[...], MLA_Q_RANK).astype(BF16)
    q = _dot(qa, wuq_ref[...])
    nq = MLA_HEADS * MLA_NOPE
    qpe = q[:, nq:nq + LANE] * cos + q[:, nq + LANE:nq + 2 * LANE] * sin
    qpe_b[...] = qpe.astype(BF16)
    for hh in range(MLA_HEADS):
        qn = q[:, hh * MLA_NOPE:(hh + 1) * MLA_NOPE].astype(BF16)
        qlat_b[:, hh * MLA_KV_RANK:(hh + 1) * MLA_KV_RANK] = _dot(qn, wuk_ref[hh]).astype(BF16)

    logf = _log_sigmoid(proj[:, C_FF:C_FF + LANE] + bf_ref[...])
    logf_o[...] = logf[:, :FOX_HEADS]
    logft = _log_sigmoid(_dot_nt(wfft_ref[...], xn) + bft_ref[...])

    r = lax.broadcasted_iota(jnp.int32, (tm, tm), 0)
    cidx = lax.broadcasted_iota(jnp.int32, (tm, tm), 1)
    if seg >= tm:
        same = None
    else:
        same = (r // seg) == (cidx // seg)
    lower = cidx <= r
    upper = r <= cidx
    if same is not None:
        lower = lower & same
        upper = upper & same
    lower = jnp.where(lower, 1.0, 0.0).astype(BF16)
    upper = jnp.where(upper, 1.0, 0.0).astype(BF16)

    hi, mid, lo = _split3(logf)
    c3 = _dot(lower, jnp.concatenate([hi, mid, lo], axis=1))
    c = (c3[:, :LANE] + c3[:, LANE:2 * LANE]) + c3[:, 2 * LANE:]
    hi, mid, lo = _split3(logft)
    ct3 = _dot(jnp.concatenate([hi, mid, lo], axis=0), upper)
    ct = (ct3[:8] + ct3[8:16]) + ct3[16:]

    if seg > tm:
        steps = seg // tm

        @pl.when(i % steps == 0)
        def _():
            carry_ref[...] = jnp.zeros_like(carry_ref)
            carryt_ref[...] = jnp.zeros_like(carryt_ref)

        c = c + carry_ref[...]
        ct = ct + carryt_ref[...]
        carry_ref[...] = c[tm - 1:tm, :]
        carryt_ref[...] = ct[:, tm - 1:tm]
    c_o[...] = c[:, :8]
    ct_o[...] = ct


def _in_proj(h, lw, cos, sin, seg, tm):
    n = h.shape[0]
    assert n % tm == 0 and (seg % tm == 0 or tm % seg == 0)
    grid = (n // tm,)

    def row(w):
        return pl.BlockSpec((tm, w), lambda i: (i, 0))

    def full(a):
        nd = a.ndim
        return pl.BlockSpec(a.shape, lambda i, _nd=nd: (0,) * _nd)

    ins = [h, lw["g_mix"], lw["w_in"], lw["w_fft"], lw["b_f"], lw["b_ft"], lw["g_q"], lw["w_uq"],
           lw["g_kv"], lw["w_uk"], cos, sin]
    in_specs = [row(D_MODEL)] + [full(a) for a in ins[1:10]] + [row(LANE), row(LANE)]
    outs = [
        (KV_W, F32), (KV_W, F32), (KV_W, F32), (KV_W, F32), (MLA_KV_RANK, F32), (MLA_ROPE, F32),
        (FOX_HEADS, F32),
        (FOX_W, BF16), (KV_W, BF16), (KV_W, BF16), (SB_W, BF16), (KV_W, BF16), (KV_W, BF16),
        (MLA_HEADS * MLA_KV_RANK, BF16), (LANE, BF16), (MLA_KV_RANK, BF16), (MLA_ROPE, BF16),
        (8, F32),
    ]
    out_shape = [jax.ShapeDtypeStruct((n, w), dt) for w, dt in outs]
    out_specs = [row(w) for w, _ in outs]
    out_shape.append(jax.ShapeDtypeStruct((8, n), F32))
    out_specs.append(pl.BlockSpec((8, tm), lambda i: (0, i)))
    res = pl.pallas_call(
        functools.partial(_in_kernel, seg, tm),
        grid=grid,
        in_specs=in_specs,
        out_specs=out_specs,
        out_shape=out_shape,
        scratch_shapes=[pltpu.VMEM((1, LANE), F32), pltpu.VMEM((8, 1), F32)],
        compiler_params=_cparams(("arbitrary",)),
        name="in_proj",
    )(*ins)
    names = ["fk32", "fv32", "sk32", "sv32", "ckv32", "kpe32", "logf", "fq", "fk", "fv", "sq", "sk", "sv",
             "qlat", "qpe", "ckv", "kpe", "c", "ct"]
    return dict(zip(names, res))


def _pair_tables(nq, reverse):
    qi, kj = [], []
    for i in range(nq):
        for j in (range(i, -1, -1) if reverse else range(i + 1)):
            qi.append(i)
            kj.append(j)
    return jnp.asarray(qi, jnp.int32), jnp.asarray(kj, jnp.int32)


def _fox_kernel(tb, qi_ref, kj_ref, q_ref, k_ref, v_ref, cq_ref, ck_ref, o_ref, m_ref, l_ref, acc_ref):
    p = pl.program_id(1)
    qi = qi_ref[p]
    kj = kj_ref[p]

    @pl.when(kj == 0)
    def _():
        m_ref[...] = jnp.full_like(m_ref, NEG_INF)
        l_ref[...] = jnp.zeros_like(l_ref)
        acc_ref[...] = jnp.zeros_like(acc_ref)

    r = lax.broadcasted_iota(jnp.int32, (tb, tb), 0)
    c = lax.broadcasted_iota(jnp.int32, (tb, tb), 1)
    allowed = (c <= r) | (kj < qi)
    for hk in range(FOX_KV_HEADS):
        k = k_ref[:, hk * HEAD_DIM:(hk + 1) * HEAD_DIM]
        v = v_ref[:, hk * HEAD_DIM:(hk + 1) * HEAD_DIM]
        for g in range(GROUP):
            h = hk * GROUP + g
            q = q_ref[:, h * HEAD_DIM:(h + 1) * HEAD_DIM]
            s = _dot_nt(q, k) + (cq_ref[:, h:h + 1] - ck_ref[h:h + 1, :])
            s = jnp.where(allowed, s, NEG_INF)
            m_prev = m_ref[h]
            m_new = jnp.maximum(m_prev, jnp.max(s, axis=1, keepdims=True))
            alpha = jnp.exp(m_prev - m_new)
            pm = jnp.exp(s - m_new)
            l_ref[h] = alpha * l_ref[h] + jnp.sum(pm, axis=1, keepdims=True)
            acc_ref[h] = alpha * acc_ref[h] + _dot(pm.astype(BF16), v)
            m_ref[h] = m_new

    @pl.when(kj == qi)
    def _():
        for h in range(FOX_HEADS):
            o_ref[:, h * HEAD_DIM:(h + 1) * HEAD_DIM] = acc_ref[h] / l_ref[h]


def _mla_kernel(tb, qi_ref, kj_ref, ql_ref, qp_ref, ckv_ref, kpe_ref, wuv_ref, o_ref, m_ref, l_ref, acc_ref):
    p = pl.program_id(1)
    qi = qi_ref[p]
    kj = kj_ref[p]

    @pl.when(kj == 0)
    def _():
        m_ref[...] = jnp.full_like(m_ref, NEG_INF)
        l_ref[...] = jnp.zeros_like(l_ref)
        acc_ref[...] = jnp.zeros_like(acc_ref)

    r = lax.broadcasted_iota(jnp.int32, (tb, tb), 0)
    c = lax.broadcasted_iota(jnp.int32, (tb, tb), 1)
    allowed = (c <= r) | (kj < qi)
    ckv = ckv_ref[...]
    kpe = kpe_ref[...]
    for h in range(MLA_HEADS):
        ql = ql_ref[:, h * MLA_KV_RANK:(h + 1) * MLA_KV_RANK]
        qp = qp_ref[:, h * MLA_ROPE:(h + 1) * MLA_ROPE]
        s = (_dot_nt(ql, ckv) + _dot_nt(qp, kpe)) * MLA_SCALE
        s = jnp.where(allowed, s, NEG_INF)
        m_prev = m_ref[h]
        m_new = jnp.maximum(m_prev, jnp.max(s, axis=1, keepdims=True))
        alpha = jnp.exp(m_prev - m_new)
        pm = jnp.exp(s - m_new)
        l_ref[h] = alpha * l_ref[h] + jnp.sum(pm, axis=1, keepdims=True)
        acc_ref[h] = alpha * acc_ref[h] + _dot(pm.astype(BF16), ckv)
        m_ref[h] = m_new

    @pl.when(kj == qi)
    def _():
        for h in range(MLA_HEADS):
            o_lat = (acc_ref[h] / l_ref[h]).astype(BF16)
            o_ref[:, h * MLA_V:(h + 1) * MLA_V] = _dot(o_lat, wuv_ref[h])


def _sb_kernel(tb, qi_ref, kj_ref, q_ref, k_ref, v_ref, o_ref, acc_ref, carry_ref):
    p = pl.program_id(1)
    qi = qi_ref[p]
    kj = kj_ref[p]

    @pl.when(kj == qi)
    def _():
        acc_ref[...] = jnp.zeros_like(acc_ref)
        carry_ref[...] = jnp.zeros_like(carry_ref)

    r = lax.broadcasted_iota(jnp.int32, (tb, tb), 0)
    c = lax.broadcasted_iota(jnp.int32, (tb, tb), 1)
    strict = (c < r) | (kj < qi)
    later = jnp.where(r > c, 1.0, 0.0).astype(BF16)
    for hk in range(SB_KV_HEADS):
        k = k_ref[:, hk * HEAD_DIM:(hk + 1) * HEAD_DIM]
        v = v_ref[:, hk * HEAD_DIM:(hk + 1) * HEAD_DIM]
        for g in range(GROUP):
            h = hk * GROUP + g
            q = q_ref[:, h * HEAD_DIM:(h + 1) * HEAD_DIM]
            z = _dot_nt(q, k)
            ls = jnp.minimum(z, 0.0) - jnp.log(1.0 + jnp.exp(-jnp.abs(z)))
            lk = jnp.where(strict, ls - z, 0.0)
            hi, lo = _split2(lk)
            sf = _dot(jnp.concatenate([hi, lo], axis=0), later)
            sfx = sf[:tb] + sf[tb:]
            a = jnp.where(strict, jnp.exp(ls + (sfx + carry_ref[h])), 0.0)
            acc_ref[h] = acc_ref[h] + _dot(a.astype(BF16), v)
            carry_ref[h] = carry_ref[h] + (sfx[:, 0:1] + lk[:, 0:1])

    @pl.when(kj == 0)
    def _():
        for h in range(SB_HEADS):
            o_ref[:, h * HEAD_DIM:(h + 1) * HEAD_DIM] = acc_ref[h]


def _prompt_attn(kind, bsz, seq, tb, ops, wuv=None):
    nq = seq // tb
    qi_t, kj_t = _pair_tables(nq, reverse=(kind == "sb"))
    npairs = int(qi_t.shape[0])
    n = bsz * seq

    def qspec(w):
        return pl.BlockSpec((tb, w), lambda b, p, qi, kj: (b * nq + qi[p], 0))

    def kspec(w):
        return pl.BlockSpec((tb, w), lambda b, p, qi, kj: (b * nq + kj[p], 0))

    if kind == "fox":
        q, k, v, cq, ct = ops
        in_specs = [qspec(FOX_W), kspec(KV_W), kspec(KV_W), qspec(8),
                    pl.BlockSpec((8, tb), lambda b, p, qi, kj: (0, b * nq + kj[p]))]
        body = functools.partial(_fox_kernel, tb)
        out_w = FOX_W
        scratch = [pltpu.VMEM((FOX_HEADS, tb, 1), F32), pltpu.VMEM((FOX_HEADS, tb, 1), F32),
                   pltpu.VMEM((FOX_HEADS, tb, HEAD_DIM), F32)]
        args = (q, k, v, cq, ct)
    elif kind == "mla":
        ql, qp, ckv, kpe = ops
        in_specs = [qspec(MLA_HEADS * MLA_KV_RANK), qspec(LANE), kspec(MLA_KV_RANK), kspec(MLA_ROPE),
                    pl.BlockSpec(wuv.shape, lambda b, p, qi, kj: (0, 0, 0))]
        body = functools.partial(_mla_kernel, tb)
        out_w = MLA_W
        scratch = [pltpu.VMEM((MLA_HEADS, tb, 1), F32), pltpu.VMEM((MLA_HEADS, tb, 1), F32),
                   pltpu.VMEM((MLA_HEADS, tb, MLA_KV_RANK), F32)]
        args = (ql, qp, ckv, kpe, wuv)
    else:
        q, k, v = ops
        in_specs = [qspec(SB_W), kspec(KV_W), kspec(KV_W)]
        body = functools.partial(_sb_kernel, tb)
        out_w = SB_W
        scratch = [pltpu.VMEM((SB_HEADS, tb, HEAD_DIM), F32), pltpu.VMEM((SB_HEADS, tb, 1), F32)]
        args = (q, k, v)

    return pl.pallas_call(
        body,
        grid_spec=pltpu.PrefetchScalarGridSpec(
            num_scalar_prefetch=2,
            grid=(bsz, npairs),
            in_specs=in_specs,
            out_specs=qspec(out_w),
            scratch_shapes=scratch),
        out_shape=jax.ShapeDtypeStruct((n, out_w), F32),
        compiler_params=_cparams(("arbitrary", "arbitrary")),
        name=kind + "_attn",
    )(qi_t, kj_t, *args)


L_MLA = FOX_HEADS * 8
L_SB = L_MLA + MLA_HEADS * 8
QW_ROWS = 3 * LANE + MLA_ROPE


def _sample_kernel(pp, nch, pt_ref, w_ref, nfk, nfv, nckv, nkpe, nsk, nsv, nlogf, *rest):
    del pt_ref
    caches = rest[:7 * pp]
    o_ref = rest[7 * pp]
    m_ref, l_ref, acc_ref, carry_ref = rest[7 * pp + 1:]
    cstep = pl.program_id(1)

    lane = lax.broadcasted_iota(jnp.int32, (1, LANE), 1)
    sb_lane = lane >= L_SB
    mla_lane = (lane >= L_MLA) & (lane < L_SB)
    scale_row = jnp.where(mla_lane, MLA_SCALE, 1.0)
    tok = lane % 8
    head_sel = [jnp.where((lane // 8) == h, 1.0, 0.0) for h in range(FOX_HEADS)]
    wf = w_ref[0, 0:LANE, :]
    wm1 = w_ref[0, LANE:2 * LANE, :]
    wm2 = w_ref[0, 2 * LANE:2 * LANE + MLA_ROPE, :]
    ws = w_ref[0, 2 * LANE + MLA_ROPE:QW_ROWS, :]

    def block(kf, ckv, kpe, ks, vf, vs, logf6, n, new):
        ckv_b = ckv.astype(BF16)
        s = (_dot(kf.astype(BF16), wf) + _dot(ckv_b, wm1) + _dot(kpe.astype(BF16), wm2)
             + _dot(ks.astype(BF16), ws)) * scale_row
        ls = jnp.minimum(s, 0.0) - jnp.log(1.0 + jnp.exp(-jnp.abs(s)))
        lk = ls - s
        if new:
            key = lax.broadcasted_iota(jnp.int32, (n, LANE), 0)
            lane2 = lax.broadcasted_iota(jnp.int32, (n, LANE), 1)
            tok2 = lane2 % 8
            strict = key < tok2
            allow = strict | ((key == tok2) & (lane2 < L_SB))
            lk = jnp.where(strict, lk, 0.0)
        lf = jnp.zeros((n, LANE), F32)
        for h in range(FOX_HEADS):
            lf = lf + logf6[:, h:h + 1] * head_sel[h]
        y = jnp.where(sb_lane, lk, lf)
        hi, lo = _split2(y)
        rr = lax.broadcasted_iota(jnp.int32, (n, n), 0)
        cc = lax.broadcasted_iota(jnp.int32, (n, n), 1)
        later = jnp.where(cc > rr, 1.0, 0.0).astype(BF16)
        sf = _dot(later, jnp.concatenate([hi, lo], axis=1))
        sfx = (sf[:, :LANE] + sf[:, LANE:]) + carry_ref[...]
        carry_ref[...] = sfx[0:1, :] + y[0:1, :]
        t = jnp.where(sb_lane, ls, s) + sfx
        if new:
            t = jnp.where(allow, t, NEG_INF)
        m_prev = m_ref[...]
        m_new = jnp.where(sb_lane, 0.0, jnp.maximum(m_prev, jnp.max(t, axis=0, keepdims=True)))
        alpha = jnp.exp(m_prev - m_new)
        p = jnp.exp(t - m_new)
        l_ref[...] = alpha * l_ref[...] + jnp.sum(p, axis=0, keepdims=True)
        pb = p.astype(BF16)
        acc_ref[0:LANE, :] = alpha * acc_ref[0:LANE, :] + _dot_tn(vf.astype(BF16), pb)
        acc_ref[LANE:2 * LANE, :] = alpha * acc_ref[LANE:2 * LANE, :] + _dot_tn(ckv_b, pb)
        acc_ref[2 * LANE:3 * LANE, :] = alpha * acc_ref[2 * LANE:3 * LANE, :] + _dot_tn(vs.astype(BF16), pb)
        m_ref[...] = m_new

    @pl.when(cstep == 0)
    def _():
        m_ref[...] = jnp.where(sb_lane, 0.0, NEG_INF)
        l_ref[...] = jnp.zeros_like(l_ref)
        acc_ref[...] = jnp.zeros_like(acc_ref)
        carry_ref[...] = jnp.zeros_like(carry_ref)
        block(nfk[0], nckv[0], nkpe[0], nsk[0], nfv[0], nsv[0], nlogf[0], 8, True)

    for i in range(pp - 1, -1, -1):
        fk, fv, lg, ck, kp, sk, sv = (caches[a * pp + i] for a in range(7))
        block(fk[...], ck[...], kp[...], sk[...], fv[...], sv[...], lg[...], PAGE, False)

    @pl.when(cstep == nch - 1)
    def _():
        o_ref[0] = acc_ref[...] * jnp.where(sb_lane, 1.0, 1.0 / l_ref[...])


def _sample_attn(layer, page_table, qw, new_rows, caches, pp):
    s_cnt, n_pages = page_table.shape
    assert n_pages % pp == 0
    nch = n_pages // pp

    def new_spec(a):
        return pl.BlockSpec((1,) + a.shape[1:], lambda b, c, pt: (b, 0, 0))

    in_specs = [pl.BlockSpec((1, QW_ROWS, LANE), lambda b, c, pt: (b, 0, 0))]
    in_specs += [new_spec(a) for a in new_rows]
    cache_args = []
    for a in caches:
        for i in range(pp):
            in_specs.append(pl.BlockSpec(
                (None, None, PAGE, a.shape[-1]),
                lambda b, c, pt, _i=i: (layer, pt[b, (nch - 1 - c) * pp + _i], 0, 0)))
            cache_args.append(a)
    return pl.pallas_call(
        functools.partial(_sample_kernel, pp, nch),
        grid_spec=pltpu.PrefetchScalarGridSpec(
            num_scalar_prefetch=1,
            grid=(s_cnt, nch),
            in_specs=in_specs,
            out_specs=pl.BlockSpec((1, 3 * LANE, LANE), lambda b, c, pt: (b, 0, 0)),
            scratch_shapes=[pltpu.VMEM((1, LANE), F32), pltpu.VMEM((1, LANE), F32),
                            pltpu.VMEM((3 * LANE, LANE), F32), pltpu.VMEM((1, LANE), F32)]),
        out_shape=jax.ShapeDtypeStruct((s_cnt, 3 * LANE, LANE), F32),
        compiler_params=_cparams(("arbitrary", "arbitrary")),
        name="sample_attn",
    )(page_table, qw, *new_rows, *cache_args)


def _sample_query_weights(fq, qlat, qpe, sq, s_cnt):
    def heads(q, nh, d):
        return q.reshape(s_cnt, 8, nh, d).transpose(0, 3, 2, 1).reshape(s_cnt, d, nh * 8)

    def gqa(q, lane0):
        a = heads(q, FOX_HEADS, HEAD_DIM)
        w = jnp.zeros((s_cnt, LANE, LANE), BF16)
        half = GROUP * 8
        w = w.at[:, :HEAD_DIM, lane0:lane0 + half].set(a[:, :, :half])
        return w.at[:, HEAD_DIM:, lane0 + half:lane0 + 2 * half].set(a[:, :, half:])

    wm1 = jnp.zeros((s_cnt, LANE, LANE), BF16).at[:, :, L_MLA:L_SB].set(heads(qlat, MLA_HEADS, MLA_KV_RANK))
    wm2 = jnp.zeros((s_cnt, MLA_ROPE, LANE), BF16).at[:, :, L_MLA:L_SB].set(heads(qpe, MLA_HEADS, MLA_ROPE))
    return jnp.concatenate([gqa(fq, 0), wm1, wm2, gqa(sq, L_SB)], axis=1)


def _sample_unpack(acc, s_cnt):
    def gqa(a):
        a = a.reshape(s_cnt, FOX_KV_HEADS, HEAD_DIM, FOX_HEADS, 8)
        parts = [a[:, hk, :, hk * GROUP:(hk + 1) * GROUP, :] for hk in range(FOX_KV_HEADS)]
        a = jnp.concatenate(parts, axis=2)
        return a.transpose(0, 3, 2, 1).reshape(s_cnt * 8, FOX_W)

    o_fox = gqa(acc[:, 0:LANE, 0:L_MLA])
    o_lat = acc[:, LANE:2 * LANE, L_MLA:L_SB].reshape(s_cnt, MLA_KV_RANK, MLA_HEADS, 8)
    o_lat = o_lat.transpose(0, 3, 2, 1).reshape(s_cnt * 8, MLA_HEADS * MLA_KV_RANK)
    o_sb = gqa(acc[:, 2 * LANE:3 * LANE, L_SB:LANE])
    return o_fox, o_lat, o_sb


def _uv_kernel(x_ref, w_ref, o_ref):
    for h in range(MLA_HEADS):
        x = x_ref[:, h * MLA_KV_RANK:(h + 1) * MLA_KV_RANK].astype(BF16)
        o_ref[:, h * MLA_V:(h + 1) * MLA_V] = _dot(x, w_ref[h])


def _uv_proj(o_lat, wuv):
    n = o_lat.shape[0]
    return pl.pallas_call(
        _uv_kernel,
        out_shape=jax.ShapeDtypeStruct((n, MLA_W), F32),
        name="uv_proj",
    )(o_lat, wuv)


def _out_kernel(of_ref, om_ref, os_ref, h_ref, gf_ref, gm_ref, gs_ref, w_ref, o_ref):
    a = _rms(of_ref[...], gf_ref[...]).astype(BF16)
    b = _rms(om_ref[...], gm_ref[...]).astype(BF16)
    c = _rms(os_ref[...], gs_ref[...]).astype(BF16)
    y = _dot(a, w_ref[0:FOX_W, :]) + _dot(b, w_ref[FOX_W:FOX_W + MLA_W, :]) + _dot(c, w_ref[FOX_W + MLA_W:, :])
    o_ref[...] = h_ref[...] + y


def _out_proj(o_fox, o_mla, o_sb, h, lw, tm):
    n = h.shape[0]

    def row(w):
        return pl.BlockSpec((tm, w), lambda i: (i, 0))

    def full(a):
        return pl.BlockSpec(a.shape, lambda i: (0, 0))

    return pl.pallas_call(
        _out_kernel,
        grid=(n // tm,),
        in_specs=[row(FOX_W), row(MLA_W), row(SB_W), row(D_MODEL), full(lw["g_of"]), full(lw["g_om"]),
                  full(lw["g_os"]), full(lw["w_out"])],
        out_specs=row(D_MODEL),
        out_shape=jax.ShapeDtypeStruct((n, D_MODEL), F32),
        compiler_params=_cparams(("arbitrary",)),
        name="out_proj",
    )(o_fox, o_mla, o_sb, h, lw["g_of"], lw["g_om"], lw["g_os"], lw["w_out"])


def _memkv_kernel(x_ref, w_ref, k_ref, v_ref):
    y = _dot(x_ref[...].astype(BF16), w_ref[0])
    k_ref[0] = y[:, :MEM_W]
    v_ref[0] = y[:, MEM_W:]


def _mem_kv(mem, w_kv, tm):
    n = mem.shape[0]
    nl = w_kv.shape[0]
    out = jax.ShapeDtypeStruct((nl, n, MEM_W), F32)
    ospec = pl.BlockSpec((1, tm, MEM_W), lambda i, l: (l, i, 0))
    return pl.pallas_call(
        _memkv_kernel,
        grid=(n // tm, nl),
        in_specs=[pl.BlockSpec((tm, D_MODEL), lambda i, l: (i, 0)),
                  pl.BlockSpec((1, D_MODEL, 2 * MEM_W), lambda i, l: (l, 0, 0))],
        out_specs=[ospec, ospec],
        out_shape=[out, out],
        compiler_params=_cparams(("arbitrary", "arbitrary")),
        name="mem_kv",
    )(mem, w_kv)


def _mem_kernel(ns, tq, h_ref, g_ref, wq_ref, k_ref, v_ref, wo_ref, o_ref):
    h = h_ref[...]
    hn = _rms(h, g_ref[...]).astype(BF16)
    q = _dot(hn, wq_ref[...]) * MEM_SCALE
    outs = []
    for hd in range(MEM_HEADS):
        sl = slice(hd * MEM_HEAD_DIM, (hd + 1) * MEM_HEAD_DIM)
        qh = q[:, sl].reshape(ns, tq, MEM_HEAD_DIM).astype(BF16)
        kh = k_ref[:, :, sl].astype(BF16)
        vh = v_ref[:, :, sl].astype(BF16)
        s = jnp.einsum("sqd,skd->sqk", qh, kh, preferred_element_type=F32)
        m = jnp.max(s, axis=-1, keepdims=True)
        p = jnp.exp(s - m)
        p = p / jnp.sum(p, axis=-1, keepdims=True)
        o = jnp.einsum("sqk,skd->sqd", p.astype(BF16), vh, preferred_element_type=F32)
        outs.append(o.reshape(ns * tq, MEM_HEAD_DIM))
    o = jnp.concatenate(outs, axis=1).astype(BF16)
    o_ref[...] = h + _dot(o, wo_ref[...])


def _mem_attn(h, mem_k, mem_v, lw, ns, tq, per_seq):
    n = h.shape[0]
    tm = ns * tq
    assert per_seq % tq == 0 and (ns == 1 or tq == per_seq)
    spb = per_seq // tq
    mt = mem_k.shape[1]

    def kv_index(i):
        return (i // spb if ns == 1 else i, 0, 0)

    def full(a):
        return pl.BlockSpec(a.shape, lambda i: (0, 0))

    return pl.pallas_call(
        functools.partial(_mem_kernel, ns, tq),
        grid=(n // tm,),
        in_specs=[pl.BlockSpec((tm, D_MODEL), lambda i: (i, 0)), full(lw["g_mem"]), full(lw["w_mq"]),
                  pl.BlockSpec((ns, mt, MEM_W), kv_index), pl.BlockSpec((ns, mt, MEM_W), kv_index),
                  full(lw["w_mo"])],
        out_specs=pl.BlockSpec((tm, D_MODEL), lambda i: (i, 0)),
        out_shape=jax.ShapeDtypeStruct((n, D_MODEL), F32),
        compiler_params=_cparams(("arbitrary",)),
        name="mem_attn",
    )(h, lw["g_mem"], lw["w_mq"], mem_k, mem_v, lw["w_mo"])


def _mlp_kernel(final, h_ref, g_ref, wu_ref, wd_ref, gfin_ref, o_ref, hn_ref, acc_ref):
    j = pl.program_id(1)

    @pl.when(j == 0)
    def _():
        hn_ref[...] = _rms(h_ref[...], g_ref[...]).astype(BF16)
        acc_ref[...] = h_ref[...]

    u = jnp.maximum(_dot(hn_ref[...], wu_ref[...]), 0.0)
    acc_ref[...] += _dot((u * u).astype(BF16), wd_ref[...])

    @pl.when(j == pl.num_programs(1) - 1)
    def _():
        y = acc_ref[...]
        o_ref[...] = _rms(y, gfin_ref[...]) if final else y


def _mlp(h, lw, g_final, final, tm, tf):
    n = h.shape[0]
    return pl.pallas_call(
        functools.partial(_mlp_kernel, final),
        grid=(n // tm, D_FF // tf),
        in_specs=[pl.BlockSpec((tm, D_MODEL), lambda i, j: (i, 0)),
                  pl.BlockSpec((1, D_MODEL), lambda i, j: (0, 0)),
                  pl.BlockSpec((D_MODEL, tf), lambda i, j: (0, j)),
                  pl.BlockSpec((tf, D_MODEL), lambda i, j: (j, 0)),
                  pl.BlockSpec((1, D_MODEL), lambda i, j: (0, 0))],
        out_specs=pl.BlockSpec((tm, D_MODEL), lambda i, j: (i, 0)),
        out_shape=jax.ShapeDtypeStruct((n, D_MODEL), F32),
        scratch_shapes=[pltpu.VMEM((tm, D_MODEL), BF16), pltpu.VMEM((tm, D_MODEL), F32)],
        compiler_params=_cparams(("arbitrary", "arbitrary")),
        name="mlp",
    )(h, lw["g_mlp"], lw["w_up"], lw["w_down"], g_final)


def _swap_halves(w, width):
    lead = w.shape[:-1]
    g = w.reshape(lead + (-1, 2, width // 2))
    return g[..., ::-1, :].reshape(w.shape)


def _prep_layer(l, w_in, b_f, g_mix, g_q, w_uq, g_kv, w_uk, w_uv, g_of, g_om, g_os, w_out,
                g_mem, w_mq, w_mo, g_mlp, w_up, w_down):
    sizes = (FOX_W, KV_W, KV_W, FOX_HEADS, MLA_Q_RANK, MLA_KV_RANK, MLA_ROPE, SB_W, KV_W, KV_W)
    pts, acc = [], 0
    for s in sizes[:-1]:
        acc += s
        pts.append(acc)
    fq, fk, fv, ff, mqa, mkv, mkpe, sq, sk, sv = jnp.split(w_in[l], pts, axis=1)
    d = w_in.shape[1]

    def z(w):
        return jnp.zeros((d, w), F32)

    w_in_r = jnp.concatenate(
        [fq, fk, fv, sq, sk, sv, mkv, mqa, z(2 * LANE - MLA_Q_RANK),
         mkpe, _swap_halves(mkpe, MLA_ROPE), z(LANE - 2 * MLA_ROPE), ff, z(LANE - FOX_HEADS)], axis=1)
    assert w_in_r.shape[1] == N_INR
    uq = w_uq[l].reshape(MLA_Q_RANK, MLA_HEADS, MLA_NOPE + MLA_ROPE)
    nope = uq[:, :, :MLA_NOPE].reshape(MLA_Q_RANK, MLA_HEADS * MLA_NOPE)
    pe = uq[:, :, MLA_NOPE:].reshape(MLA_Q_RANK, MLA_HEADS * MLA_ROPE)
    w_uq_r = jnp.concatenate([nope, pe, _swap_halves(pe, MLA_ROPE)], axis=1)
    w_uq_r = jnp.pad(w_uq_r, ((0, 2 * LANE - MLA_Q_RANK), (0, 0)))
    return dict(
        g_mix=g_mix[l][None, :],
        w_in=w_in_r.astype(BF16),
        w_fft=jnp.pad(ff.T, ((0, 8 - FOX_HEADS), (0, 0))).astype(BF16),
        b_f=jnp.pad(b_f[l], (0, LANE - FOX_HEADS))[None, :],
        b_ft=jnp.pad(b_f[l], (0, 8 - FOX_HEADS))[:, None],
        g_q=jnp.pad(g_q[l], (0, 2 * LANE - MLA_Q_RANK))[None, :],
        w_uq=w_uq_r.astype(BF16),
        g_kv=g_kv[l][None, :],
        w_uk=w_uk[l].transpose(1, 2, 0).astype(BF16),
        w_uv=w_uv[l].transpose(1, 0, 2).astype(BF16),
        g_of=g_of[l][None, :], g_om=g_om[l][None, :], g_os=g_os[l][None, :],
        w_out=w_out[l].astype(BF16),
        g_mem=g_mem[l][None, :], w_mq=w_mq[l].astype(BF16), w_mo=w_mo[l].astype(BF16),
        g_mlp=g_mlp[l][None, :], w_up=w_up[l].astype(BF16), w_down=w_down[l].astype(BF16),
    )


def _rope_tables(pos):
    half = MLA_ROPE // 2
    inv_freq = jnp.power(ROPE_BASE, -jnp.arange(half, dtype=F32) / half)
    ang = pos.astype(F32)[:, None] * inv_freq[None, :]
    cos, sin = jnp.cos(ang), jnp.sin(ang)
    cos = jnp.tile(jnp.concatenate([cos, cos], axis=1), (1, MLA_HEADS))
    sin = jnp.tile(jnp.concatenate([-sin, sin], axis=1), (1, MLA_HEADS))
    return cos, sin


TM_IN = 512
TB_ATT = 256
TM_MLP = 1024
TF_MLP = 1024
TQ_MEM = 512
NS_MEM = 16
PP = 8


def kernel(x_prompt, x_sample, cache_fox_k, cache_fox_v, cache_fox_logf, cache_mla_ckv, cache_mla_kpe,
           cache_sb_k, cache_sb_v, cache_mem_k, cache_mem_v, page_table, mem_prompt,
           g_mix, w_in, b_fox_f, g_mla_q, w_mla_uq, g_mla_kv, w_mla_uk, w_mla_uv,
           g_out_fox, g_out_mla, g_out_sb, w_out, g_mem, w_mem_q, w_mem_k, w_mem_v, w_mem_o,
           g_mlp, w_up, w_down, g_final):
    bsz, seq, _ = x_prompt.shape
    s_cnt, n_new, _ = x_sample.shape
    depth = w_in.shape[0]
    past_len = page_table.shape[1] * PAGE
    assert n_new == 8
    n_p, n_s = bsz * seq, s_cnt * n_new

    cos_p, sin_p = _rope_tables(jnp.arange(seq, dtype=jnp.int32))
    cos_p, sin_p = jnp.tile(cos_p, (bsz, 1)), jnp.tile(sin_p, (bsz, 1))
    cos_s, sin_s = _rope_tables(past_len + jnp.arange(n_new, dtype=jnp.int32))
    cos_s, sin_s = jnp.tile(cos_s, (s_cnt, 1)), jnp.tile(sin_s, (s_cnt, 1))

    pool = cache_fox_k.shape[1]
    caches = (cache_fox_k.reshape(depth, pool, PAGE, KV_W), cache_fox_v.reshape(depth, pool, PAGE, KV_W),
              cache_fox_logf, cache_mla_ckv, cache_mla_kpe,
              cache_sb_k.reshape(depth, pool, PAGE, KV_W), cache_sb_v.reshape(depth, pool, PAGE, KV_W))
    mem_tok = mem_prompt.shape[1]
    w_kv = jnp.concatenate([w_mem_k, w_mem_v], axis=2).astype(BF16)
    p_mem_k, p_mem_v = _mem_kv(mem_prompt.reshape(bsz * mem_tok, D_MODEL), w_kv, 512)
    s_mem_k = cache_mem_k.reshape(depth, s_cnt, mem_tok, MEM_W)
    s_mem_v = cache_mem_v.reshape(depth, s_cnt, mem_tok, MEM_W)
    g_fin = g_final[None, :]

    hp = x_prompt.reshape(n_p, D_MODEL)
    hs = x_sample.reshape(n_s, D_MODEL)
    rows_p, rows_s = [], []
    tm_p, tm_s = min(TM_IN, seq), min(TM_IN, n_s)
    tmlp_p, tmlp_s = min(TM_MLP, n_p), min(TM_MLP, n_s)
    tb = min(TB_ATT, seq)
    tq_mem = min(TQ_MEM, seq)
    ns_mem = min(NS_MEM, s_cnt)
    pp = min(PP, page_table.shape[1])
    for l in range(depth):
        lw = _prep_layer(l, w_in, b_fox_f, g_mix, g_mla_q, w_mla_uq, g_mla_kv, w_mla_uk, w_mla_uv,
                         g_out_fox, g_out_mla, g_out_sb, w_out, g_mem, w_mem_q, w_mem_o, g_mlp, w_up, w_down)
        final = l == depth - 1

        r = _in_proj(hp, lw, cos_p, sin_p, seq, tm_p)
        o_fox = _prompt_attn("fox", bsz, seq, tb, (r["fq"], r["fk"], r["fv"], r["c"], r["ct"]))
        o_mla = _prompt_attn("mla", bsz, seq, tb, (r["qlat"], r["qpe"], r["ckv"], r["kpe"]), lw["w_uv"])
        o_sb = _prompt_attn("sb", bsz, seq, tb, (r["sq"], r["sk"], r["sv"]))
        hp = _out_proj(o_fox, o_mla, o_sb, hp, lw, tm_p)
        hp = _mem_attn(hp, p_mem_k[l].reshape(bsz, mem_tok, MEM_W), p_mem_v[l].reshape(bsz, mem_tok, MEM_W),
                       lw, 1, tq_mem, seq)
        hp = _mlp(hp, lw, g_fin, final, tmlp_p, TF_MLP)
        rows_p.append(r)

        r = _in_proj(hs, lw, cos_s, sin_s, n_new, tm_s)
        qw = _sample_query_weights(r["fq"], r["qlat"], r["qpe"], r["sq"], s_cnt)
        new_rows = [r[k].reshape(s_cnt, n_new, -1) for k in ("fk32", "fv32", "ckv32", "kpe32", "sk32", "sv32", "logf")]
        acc = _sample_attn(l, page_table, qw, new_rows, caches, pp)
        o_fox, o_lat, o_sb = _sample_unpack(acc, s_cnt)
        o_mla = _uv_proj(o_lat, lw["w_uv"])
        hs = _out_proj(o_fox, o_mla, o_sb, hs, lw, tm_s)
        hs = _mem_attn(hs, s_mem_k[l], s_mem_v[l], lw, ns_mem, n_new, n_new)
        hs = _mlp(hs, lw, g_fin, final, tmlp_s, TF_MLP)
        rows_s.append(r)

    def stack(rows, key, shape):
        return jnp.stack([r[key] for r in rows]).reshape((depth,) + shape)

    def group_rows(rows, lead):
        return (stack(rows, "fk32", lead + (FOX_KV_HEADS, HEAD_DIM)), stack(rows, "fv32", lead + (FOX_KV_HEADS, HEAD_DIM)),
                stack(rows, "logf", lead + (FOX_HEADS,)), stack(rows, "ckv32", lead + (MLA_KV_RANK,)),
                stack(rows, "kpe32", lead + (MLA_ROPE,)),
                stack(rows, "sk32", lead + (SB_KV_HEADS, HEAD_DIM)), stack(rows, "sv32", lead + (SB_KV_HEADS, HEAD_DIM)))

    y_prompt = hp.reshape(bsz, seq, D_MODEL)
    y_sample = hs.reshape(s_cnt, n_new, D_MODEL)
    pm_shape = (depth, bsz, mem_tok, MEM_HEADS, MEM_HEAD_DIM)
    return ((y_prompt, y_sample) + group_rows(rows_p, (bsz, seq))
            + (p_mem_k.reshape(pm_shape), p_mem_v.reshape(pm_shape)) + group_rows(rows_s, (s_cnt, n_new)))
```

```python
import functools
import math

import jax
import jax.numpy as jnp
import numpy as np
from jax import lax
from jax.experimental import pallas as pl
from jax.experimental.pallas import tpu as pltpu

F32 = jnp.float32
BF16 = jnp.bfloat16

D_MODEL = 1024
HEAD_DIM = 64
FOX_HEADS = 6
FOX_KV_HEADS = 2
SB_HEADS = 6
SB_KV_HEADS = 2
GROUP = 3
MLA_HEADS = 4
MLA_Q_RANK = 192
MLA_KV_RANK = 128
MLA_NOPE = 64
MLA_ROPE = 32
MLA_V = 64
ROPE_BASE = 10000.0
MEM_HEADS = 4
MEM_HEAD_DIM = 64
MEM_W = MEM_HEADS * MEM_HEAD_DIM
D_FF = 4 * D_MODEL
EPS = 1e-6
PAGE = 128
N_NEW = 8
FOX_W = FOX_HEADS * HEAD_DIM
MLA_W = MLA_HEADS * MLA_V
SB_W = SB_HEADS * HEAD_DIM
KV_W = FOX_KV_HEADS * HEAD_DIM
ATT_SCALE = 1.0 / math.sqrt(HEAD_DIM)
MLA_SCALE = 1.0 / math.sqrt(MLA_NOPE + MLA_ROPE)
MEM_SCALE = 1.0 / math.sqrt(MEM_HEAD_DIM)
MLA_EXP2 = MLA_SCALE * math.log2(math.e)

LANE = 128
MXU = 256
VMEM_LIMIT = 48 * 1024 * 1024

X_C = HEAD_DIM
X_G = HEAD_DIM + 3
X_ONE_V = HEAD_DIM
X_ONE_M = MLA_KV_RANK + MLA_ROPE

C_FK = 0
C_FV = C_FK + KV_W
C_SK = C_FV + KV_W
C_SV = C_SK + KV_W
C_CKV = C_SV + KV_W
C_KPE = C_CKV + MLA_KV_RANK
C_KPS = C_KPE + LANE
C_FF = C_KPS + LANE
C_MQA = C_FF + LANE
C_PQF = C_MQA + 2 * LANE
C_PKF = C_PQF + FOX_HEADS * LANE
C_PVF = C_PKF + FOX_KV_HEADS * LANE
C_PQS = C_PVF + FOX_KV_HEADS * LANE
C_PKS = C_PQS + SB_HEADS * LANE
C_PVS = C_PKS + SB_KV_HEADS * LANE
N_INR = C_PVS + SB_KV_HEADS * LANE

U_NOPE = 0
U_PE = MLA_HEADS * MLA_NOPE
U_PS = U_PE + MLA_HEADS * LANE
N_UQ = U_PS + MLA_HEADS * LANE

NEG_INF = float("-inf")


def _cparams(sem):
    return pltpu.CompilerParams(dimension_semantics=sem, vmem_limit_bytes=VMEM_LIMIT)


def _log_sigmoid(x):
    return jnp.minimum(x, 0.0) - jnp.log(1.0 + jnp.exp(-jnp.abs(x)))


def _split3(x):
    hi = x.astype(BF16)
    r = x - hi.astype(F32)
    mid = r.astype(BF16)
    lo = (r - mid.astype(F32)).astype(BF16)
    return hi, mid, lo


def _split2(x):
    hi = x.astype(BF16)
    lo = (x - hi.astype(F32)).astype(BF16)
    return hi, lo


def _dot(a, b):
    return jnp.dot(a, b, preferred_element_type=F32)


def _dot_nt(a, b):
    return lax.dot_general(a, b, (((1,), (1,)), ((), ())), preferred_element_type=F32)


def _rms(x, g, n=None):
    n = x.shape[-1] if n is None else n
    ms = jnp.sum(x * x, axis=-1, keepdims=True) * (1.0 / n)
    return x * lax.rsqrt(ms + EPS) * g


def _in_kernel(seg, tm, h_ref, g_ref, w_ref, wfft_ref, bf_ref, bft_ref, gq_ref, wuq_ref, gkv_ref,
               wuk_ref, cos_ref, sin_ref, sel_ref, one_ref,
               fk32, fv32, sk32, sv32, ckv32, kpe32, logf_o, logft_o,
               qf_o, kf_o, vf_o, qs_o, ks_o, vs_o, qm_o, kvm_o, carry_ref):
    i = pl.program_id(0)
    xn = _rms(h_ref[...], g_ref[...]).astype(BF16)
    proj = _dot(xn, w_ref[...])

    fk32[...] = proj[:, C_FK:C_FK + KV_W]
    fv32[...] = proj[:, C_FV:C_FV + KV_W]
    sk32[...] = proj[:, C_SK:C_SK + KV_W]
    sv32[...] = proj[:, C_SV:C_SV + KV_W]

    logf = _log_sigmoid(proj[:, C_FF:C_FF + LANE] + bf_ref[...])
    logf_o[...] = logf[:, :FOX_HEADS]
    logft_o[...] = _log_sigmoid(_dot_nt(wfft_ref[...], xn) + bft_ref[...])
    r = lax.broadcasted_iota(jnp.int32, (tm, tm), 0)
    cidx = lax.broadcasted_iota(jnp.int32, (tm, tm), 1)
    lower = cidx <= r
    if seg < tm:
        lower = lower & ((r // seg) == (cidx // seg))
    lower = jnp.where(lower, 1.0, 0.0).astype(BF16)
    hi, mid, lo = _split3(logf)
    c3 = _dot(lower, jnp.concatenate([hi, mid, lo], axis=1))
    c = (c3[:, :LANE] + c3[:, LANE:2 * LANE]) + c3[:, 2 * LANE:]
    if seg > tm:
        @pl.when(i % (seg // tm) == 0)
        def _():
            carry_ref[...] = jnp.zeros_like(carry_ref)

        c = c + carry_ref[...]
        carry_ref[...] = c[tm - 1:tm, :]
    hi, mid, lo = _split3(c)
    ext = _dot(jnp.concatenate([hi, mid, lo], axis=1), sel_ref[...]) + one_ref[...]
    nq = FOX_HEADS * LANE
    nk = FOX_KV_HEADS * LANE
    qf_o[...] = (proj[:, C_PQF:C_PQF + nq] * ATT_SCALE + ext[:, :nq]).astype(BF16)
    kf_o[...] = (proj[:, C_PKF:C_PKF + nk] + ext[:, nq:nq + nk]).astype(BF16)
    vf_o[...] = (proj[:, C_PVF:C_PVF + nk] + ext[:, nq + nk:]).astype(BF16)
    qs_o[...] = (proj[:, C_PQS:C_PQS + nq] * ATT_SCALE).astype(BF16)
    ks_o[...] = proj[:, C_PKS:C_PKS + nk].astype(BF16)
    vs_o[...] = proj[:, C_PVS:C_PVS + nk].astype(BF16)

    ckv = _rms(proj[:, C_CKV:C_CKV + MLA_KV_RANK], gkv_ref[...])
    ckv32[...] = ckv
    cos = cos_ref[...]
    sin = sin_ref[...]
    kpe = proj[:, C_KPE:C_KPE + LANE] * cos + proj[:, C_KPS:C_KPS + LANE] * sin
    kpe32[...] = kpe[:, :MLA_ROPE]
    lane = lax.broadcasted_iota(jnp.int32, (1, LANE), 1)
    kvm_o[:, :LANE] = ckv.astype(BF16)
    kvm_o[:, LANE:] = (kpe + jnp.where(lane == MLA_ROPE, 1.0, 0.0)).astype(BF16)
    qa = _rms(proj[:, C_MQA:C_MQA + 2 * LANE], gq_ref[...], MLA_Q_RANK).astype(BF16)
    q = _dot(qa, wuq_ref[...])
    for hh in range(MLA_HEADS):
        qn = q[:, U_NOPE + hh * MLA_NOPE:U_NOPE + (hh + 1) * MLA_NOPE].astype(BF16)
        qm_o[:, hh * MXU:hh * MXU + LANE] = _dot(qn, wuk_ref[hh]).astype(BF16)
        qpe = (q[:, U_PE + hh * LANE:U_PE + (hh + 1) * LANE] * cos
               + q[:, U_PS + hh * LANE:U_PS + (hh + 1) * LANE] * sin)
        qm_o[:, hh * MXU + LANE:(hh + 1) * MXU] = qpe.astype(BF16)


def _in_proj(h, lw, consts, cos, sin, seg, tm):
    n = h.shape[0]
    assert n % tm == 0 and (seg % tm == 0 or tm % seg == 0)

    def row(w):
        return pl.BlockSpec((tm, w), lambda i: (i, 0))

    def full(a):
        nd = a.ndim
        return pl.BlockSpec(a.shape, lambda i, _nd=nd: (0,) * _nd)

    ins = [h, lw["g_mix"], lw["w_in"], lw["w_fft"], lw["b_f"], lw["b_ft"], lw["g_q"], lw["w_uq"],
           lw["g_kv"], lw["w_uk"], cos, sin, consts["sel"], consts["ones"]]
    in_specs = [row(D_MODEL)] + [full(a) for a in ins[1:10]] + [row(LANE), row(LANE), full(ins[12]), full(ins[13])]
    outs = [
        ("fk32", KV_W, F32), ("fv32", KV_W, F32), ("sk32", KV_W, F32), ("sv32", KV_W, F32),
        ("ckv32", MLA_KV_RANK, F32), ("kpe32", MLA_ROPE, F32), ("logf", FOX_HEADS, F32), ("logft", None, F32),
        ("qf", FOX_HEADS * LANE, BF16), ("kf", FOX_KV_HEADS * LANE, BF16), ("vf", FOX_KV_HEADS * LANE, BF16),
        ("qs", SB_HEADS * LANE, BF16), ("ks", SB_KV_HEADS * LANE, BF16), ("vs", SB_KV_HEADS * LANE, BF16),
        ("qm", MLA_HEADS * MXU, BF16), ("kvm", MXU, BF16),
    ]
    out_shape, out_specs = [], []
    for _, w, dt in outs:
        if w is None:
            out_shape.append(jax.ShapeDtypeStruct((8, n), dt))
            out_specs.append(pl.BlockSpec((8, tm), lambda i: (0, i)))
        else:
            out_shape.append(jax.ShapeDtypeStruct((n, w), dt))
            out_specs.append(row(w))
    res = pl.pallas_call(
        functools.partial(_in_kernel, seg, tm),
        grid=(n // tm,),
        in_specs=in_specs,
        out_specs=out_specs,
        out_shape=out_shape,
        scratch_shapes=[pltpu.VMEM((1, LANE), F32)],
        compiler_params=_cparams(("arbitrary",)),
        name="in_proj",
    )(*ins)
    return dict(zip([o[0] for o in outs], res))


def _pair_tables(nq):
    qi, kj = [], []
    for i in range(nq):
        for j in range(i, -1, -1):
            qi.append(i)
            kj.append(j)
    return jnp.asarray(qi, jnp.int32), jnp.asarray(kj, jnp.int32)


def _heads_on_rows(ref, first, count, width):
    return jnp.concatenate([ref[:, (first + g) * width:(first + g + 1) * width] for g in range(count)], axis=0)


def _attn_kernel(tb, qi_ref, kj_ref, qf_ref, kf_ref, vf_ref, qs_ref, ks_ref, vs_ref, qm_ref, kvm_ref,
                 later_ref, wuv_ref, of_ref, om_ref, os_ref,
                 fm_ref, facc_ref, mm_ref, macc_ref, sacc_ref, scar_ref):
    p = pl.program_id(1)
    qi = qi_ref[p]
    kj = kj_ref[p]

    @pl.when(kj == qi)
    def _():
        fm_ref[...] = jnp.full_like(fm_ref, NEG_INF)
        mm_ref[...] = jnp.full_like(mm_ref, NEG_INF)
        facc_ref[...] = jnp.zeros_like(facc_ref)
        macc_ref[...] = jnp.zeros_like(macc_ref)
        sacc_ref[...] = jnp.zeros_like(sacc_ref)
        scar_ref[...] = jnp.zeros_like(scar_ref)

    def body(diag):
        def masks(rows):
            r = lax.broadcasted_iota(jnp.int32, (rows, tb), 0) % tb
            c = lax.broadcasted_iota(jnp.int32, (rows, tb), 1)
            return c <= r, c < r

        if diag:
            allow_g, strict_g = masks(GROUP * tb)
            allow_m, _ = masks(MLA_HEADS * tb)

        for hk in range(FOX_KV_HEADS):
            q = _heads_on_rows(qf_ref, hk * GROUP, GROUP, LANE)
            s = _dot_nt(q, kf_ref[:, hk * LANE:(hk + 1) * LANE])
            if diag:
                s = jnp.where(allow_g, s, NEG_INF)
            m_prev = fm_ref[hk]
            m_new = jnp.maximum(m_prev, jnp.max(s, axis=1, keepdims=True))
            pm = jnp.exp(s - m_new).astype(BF16)
            facc_ref[hk] = jnp.exp(m_prev - m_new) * facc_ref[hk] + _dot(pm, vf_ref[:, hk * LANE:(hk + 1) * LANE])
            fm_ref[hk] = m_new

        q = _heads_on_rows(qm_ref, 0, MLA_HEADS, MXU)
        kv = kvm_ref[...]
        s = _dot_nt(q, kv)
        if diag:
            s = jnp.where(allow_m, s, NEG_INF)
        m_prev = mm_ref[...]
        m_new = jnp.maximum(m_prev, jnp.max(s, axis=1, keepdims=True))
        pm = jnp.exp2((s - m_new) * MLA_EXP2).astype(BF16)
        macc_ref[...] = jnp.exp2((m_prev - m_new) * MLA_EXP2) * macc_ref[...] + _dot(pm, kv)
        mm_ref[...] = m_new

        later = later_ref[...]
        for hk in range(SB_KV_HEADS):
            q = _heads_on_rows(qs_ref, hk * GROUP, GROUP, LANE)
            z = _dot_nt(q, ks_ref[:, hk * LANE:(hk + 1) * LANE])
            ls = jnp.minimum(z, 0.0) - jnp.log(1.0 + jnp.exp(-jnp.abs(z)))
            lk = ls - z
            if diag:
                lk = jnp.where(strict_g, lk, 0.0)
            hi, lo = _split2(lk)
            sf = _dot(jnp.concatenate([hi, lo], axis=0), later)
            sfx = sf[:GROUP * tb] + sf[GROUP * tb:]
            a = jnp.exp(ls + (sfx + scar_ref[hk]))
            if diag:
                a = jnp.where(strict_g, a, 0.0)
            sacc_ref[hk] = sacc_ref[hk] + _dot(a.astype(BF16), vs_ref[:, hk * LANE:(hk + 1) * LANE])
            scar_ref[hk] = scar_ref[hk] + (sfx[:, 0:1] + lk[:, 0:1])

    @pl.when(kj == qi)
    def _():
        body(True)

    @pl.when(kj < qi)
    def _():
        body(False)

    @pl.when(kj == 0)
    def _():
        for hk in range(FOX_KV_HEADS):
            for g in range(GROUP):
                h = hk * GROUP + g
                rows = slice(g * tb, (g + 1) * tb)
                of_ref[:, h * HEAD_DIM:(h + 1) * HEAD_DIM] = (
                    facc_ref[hk, rows, 0:HEAD_DIM] / facc_ref[hk, rows, X_ONE_V:X_ONE_V + 1])
                os_ref[:, h * HEAD_DIM:(h + 1) * HEAD_DIM] = sacc_ref[hk, rows, 0:HEAD_DIM]
        for h in range(MLA_HEADS):
            rows = slice(h * tb, (h + 1) * tb)
            o_lat = (macc_ref[rows, 0:MLA_KV_RANK] / macc_ref[rows, X_ONE_M:X_ONE_M + 1]).astype(BF16)
            om_ref[:, h * MLA_V:(h + 1) * MLA_V] = _dot(o_lat, wuv_ref[h])


def _prompt_attn(bsz, seq, tb, r, later, wuv):
    nq = seq // tb
    qi_t, kj_t = _pair_tables(nq)
    n = bsz * seq

    def qspec(w):
        return pl.BlockSpec((tb, w), lambda b, p, qi, kj: (b * nq + qi[p], 0))

    def kspec(w):
        return pl.BlockSpec((tb, w), lambda b, p, qi, kj: (b * nq + kj[p], 0))

    nq_l, nk_l = FOX_HEADS * LANE, FOX_KV_HEADS * LANE
    in_specs = [qspec(nq_l), kspec(nk_l), kspec(nk_l), qspec(nq_l), kspec(nk_l), kspec(nk_l),
                qspec(MLA_HEADS * MXU), kspec(MXU),
                pl.BlockSpec(later.shape, lambda b, p, qi, kj: (0, 0)),
                pl.BlockSpec(wuv.shape, lambda b, p, qi, kj: (0, 0, 0))]
    scratch = [pltpu.VMEM((FOX_KV_HEADS, GROUP * tb, 1), F32), pltpu.VMEM((FOX_KV_HEADS, GROUP * tb, LANE), F32),
               pltpu.VMEM((MLA_HEADS * tb, 1), F32), pltpu.VMEM((MLA_HEADS * tb, MXU), F32),
               pltpu.VMEM((SB_KV_HEADS, GROUP * tb, LANE), F32), pltpu.VMEM((SB_KV_HEADS, GROUP * tb, 1), F32)]
    return pl.pallas_call(
        functools.partial(_attn_kernel, tb),
        grid_spec=pltpu.PrefetchScalarGridSpec(
            num_scalar_prefetch=2,
            grid=(bsz, int(qi_t.shape[0])),
            in_specs=in_specs,
            out_specs=[qspec(FOX_W), qspec(MLA_W), qspec(SB_W)],
            scratch_shapes=scratch),
        out_shape=[jax.ShapeDtypeStruct((n, FOX_W), F32), jax.ShapeDtypeStruct((n, MLA_W), F32),
                   jax.ShapeDtypeStruct((n, SB_W), F32)],
        compiler_params=_cparams(("arbitrary", "arbitrary")),
        name="prompt_attn",
    )(qi_t, kj_t, r["qf"], r["kf"], r["vf"], r["qs"], r["ks"], r["vs"], r["qm"], r["kvm"], later, wuv)


R_FOX = FOX_HEADS * N_NEW
R_FS = R_FOX + SB_HEADS * N_NEW
R_MLA = MLA_HEADS * N_NEW
R_CUM = SB_HEADS * N_NEW + 8
N_ARR = 7


def _sample_kernel(pp, nch, layer, pt_ref, qfs_ref, qm1_ref, qm2_ref, nk_ref, nv_ref, nckv_ref, nkpe_ref, nlf_ref,
                   later_ref, *rest):
    caches = rest[:N_ARR]
    ofs_ref, om_ref = rest[N_ARR:N_ARR + 2]
    mf_ref, lf_ref, mm_ref, lm_ref, accfs_ref, accm_ref, carry_ref = rest[N_ARR + 2:N_ARR + 9]
    bufs = rest[N_ARR + 9:2 * N_ARR + 9]
    sem = rest[2 * N_ARR + 9]
    b = pl.program_id(0)
    cstep = pl.program_id(1)
    step = b * nch + cstep
    slot = step % 2
    qfs = qfs_ref[0]
    qm1 = qm1_ref[0]
    qm2 = qm2_ref[0]

    def page_copies(bb, cc, sl):
        out = []
        for i in range(pp):
            pg = pt_ref[bb, (nch - 1 - cc) * pp + i]
            for a in range(N_ARR):
                if a == 2:
                    src = caches[a].at[layer, :, pl.ds(pl.multiple_of((pg // 8) * 8, 8), 8), :]
                else:
                    src = caches[a].at[layer, pg]
                out.append(pltpu.make_async_copy(src, bufs[a].at[sl, i], sem.at[sl, a]))
        return out

    @pl.when(step == 0)
    def _():
        for cp in page_copies(b, cstep, slot):
            cp.start()

    last_c = cstep == nch - 1
    nb = jnp.where(last_c, b + 1, b)
    nc = jnp.where(last_c, 0, cstep + 1)

    @pl.when(step + 1 < pl.num_programs(0) * nch)
    def _():
        for cp in page_copies(nb, nc, 1 - slot):
            cp.start()

    def block(s_fs, s_m, lf8, width, new, pv_fs, pv_m):
        z = s_fs[R_FOX:]
        ls = jnp.minimum(z, 0.0) - jnp.log(1.0 + jnp.exp(-jnp.abs(z)))
        lk = ls - z
        if new:
            key = lax.broadcasted_iota(jnp.int32, (R_FOX, width), 1)
            tok = lax.broadcasted_iota(jnp.int32, (R_FOX, width), 0) % N_NEW
            strict = key < tok
            allow = key <= tok
            keym = lax.broadcasted_iota(jnp.int32, (R_MLA, width), 1)
            allow_m = keym <= (lax.broadcasted_iota(jnp.int32, (R_MLA, width), 0) % N_NEW)
            lk = jnp.where(strict, lk, 0.0)
        y = jnp.concatenate([lk, lf8], axis=0)
        hi, lo = _split2(y)
        hl = jnp.concatenate([hi, lo], axis=0)
        tile = min(width, MXU)
        nt = width // tile
        later = later_ref[0:tile, 0:tile]
        sfs, tots = [], []
        for t in range(nt):
            sf = _dot(hl[:, t * tile:(t + 1) * tile], later)
            sf = sf[:R_CUM] + sf[R_CUM:]
            sfs.append(sf)
            tots.append(sf[:, 0:1] + y[:, t * tile:t * tile + 1])
        car = carry_ref[...]
        parts = [None] * nt
        for t in range(nt - 1, -1, -1):
            parts[t] = sfs[t] + car
            car = car + tots[t]
        carry_ref[...] = car
        sfx = parts[0] if nt == 1 else jnp.concatenate(parts, axis=1)
        decay = jnp.concatenate(
            [jnp.broadcast_to(sfx[R_CUM - 8 + h:R_CUM - 7 + h], (N_NEW, width)) for h in range(FOX_HEADS)], axis=0)
        tf = s_fs[:R_FOX] + decay
        a = jnp.exp(ls + sfx[:R_CUM - 8])
        if new:
            tf = jnp.where(allow, tf, NEG_INF)
            s_m = jnp.where(allow_m, s_m, NEG_INF)
            a = jnp.where(strict, a, 0.0)
        m_prev = mf_ref[...]
        m_new = jnp.maximum(m_prev, jnp.max(tf, axis=1, keepdims=True))
        alpha = jnp.exp(m_prev - m_new)
        pf = jnp.exp(tf - m_new)
        lf_ref[...] = alpha * lf_ref[...] + jnp.sum(pf, axis=1, keepdims=True)
        mf_ref[...] = m_new
        pfs = jnp.concatenate([pf, a], axis=0).astype(BF16)
        o = pv_fs(pfs)
        accfs_ref[0:R_FOX, :] = alpha * accfs_ref[0:R_FOX, :] + o[:R_FOX]
        accfs_ref[R_FOX:, :] = accfs_ref[R_FOX:, :] + o[R_FOX:]
        m_prev = mm_ref[...]
        m_new = jnp.maximum(m_prev, jnp.max(s_m, axis=1, keepdims=True))
        alpha = jnp.exp2((m_prev - m_new) * MLA_EXP2)
        pm = jnp.exp2((s_m - m_new) * MLA_EXP2)
        lm_ref[...] = alpha * lm_ref[...] + jnp.sum(pm, axis=1, keepdims=True)
        mm_ref[...] = m_new
        accm_ref[...] = alpha * accm_ref[...] + pv_m(pm.astype(BF16))

    @pl.when(cstep == 0)
    def _():
        mf_ref[...] = jnp.full_like(mf_ref, NEG_INF)
        mm_ref[...] = jnp.full_like(mm_ref, NEG_INF)
        lf_ref[...] = jnp.zeros_like(lf_ref)
        lm_ref[...] = jnp.zeros_like(lm_ref)
        accfs_ref[...] = jnp.zeros_like(accfs_ref)
        accm_ref[...] = jnp.zeros_like(accm_ref)
        carry_ref[...] = jnp.zeros_like(carry_ref)
        nk = nk_ref[0].astype(BF16)
        nv = nv_ref[0].astype(BF16)
        nckv = nckv_ref[0].astype(BF16)
        nkpe = nkpe_ref[0].astype(BF16)
        block(_dot_nt(qfs, nk), _dot_nt(qm1, nckv) + _dot_nt(qm2, nkpe), nlf_ref[0], PAGE, True,
              lambda pfs: _dot(pfs, nv), lambda pm: _dot(pm, nckv))

    for cp in page_copies(b, cstep, slot):
        cp.wait()

    def cat(a, axis):
        return jnp.concatenate([bufs[a][slot, i].astype(BF16) for i in range(pp)], axis=axis)

    kfs = jnp.concatenate([cat(0, 1), cat(5, 1)], axis=0)
    vfs = jnp.concatenate([cat(1, 1), cat(6, 1)], axis=0)
    ckv = cat(3, 0)
    kpet = cat(4, 1)
    lfs = []
    for i in range(pp):
        row = pt_ref[b, (nch - 1 - cstep) * pp + i] % 8
        six = jnp.concatenate([bufs[2][slot, i, h, pl.ds(row, 1), :] for h in range(FOX_HEADS)], axis=0)
        lfs.append(jnp.concatenate([six, jnp.zeros((8 - FOX_HEADS, PAGE), F32)], axis=0))
    lf8 = jnp.concatenate(lfs, axis=1)
    block(_dot(qfs, kfs), _dot_nt(qm1, ckv) + _dot(qm2, kpet), lf8, pp * PAGE, False,
          lambda pfs: _dot_nt(pfs, vfs), lambda pm: _dot(pm, ckv))

    @pl.when(cstep == nch - 1)
    def _():
        inv = 1.0 / lf_ref[...]
        ofs_ref[0, 0:R_FOX, :] = accfs_ref[0:R_FOX, :] * inv
        ofs_ref[0, R_FOX:, :] = accfs_ref[R_FOX:, :]
        om_ref[0] = accm_ref[...] * (1.0 / lm_ref[...])


def _sample_attn(layer, page_table, qops, new_ops, later, caches, pp):
    s_cnt, n_pages = page_table.shape
    assert n_pages % pp == 0
    nch = n_pages // pp

    def per_sample(a):
        return pl.BlockSpec((1,) + a.shape[1:], lambda b, c, pt: (b, 0, 0))

    in_specs = [per_sample(a) for a in qops] + [per_sample(a) for a in new_ops]
    in_specs.append(pl.BlockSpec(later.shape, lambda b, c, pt: (0, 0)))
    in_specs += [pl.BlockSpec(memory_space=pl.ANY)] * N_ARR
    bufs = [pltpu.VMEM((2, pp, FOX_HEADS, 8, PAGE) if k == 2 else (2, pp) + a.shape[2:], F32)
            for k, a in enumerate(caches)]
    return pl.pallas_call(
        functools.partial(_sample_kernel, pp, nch, layer),
        grid_spec=pltpu.PrefetchScalarGridSpec(
            num_scalar_prefetch=1,
            grid=(s_cnt, nch),
            in_specs=in_specs,
            out_specs=[pl.BlockSpec((1, R_FS, MXU), lambda b, c, pt: (b, 0, 0)),
                       pl.BlockSpec((1, R_MLA, LANE), lambda b, c, pt: (b, 0, 0))],
            scratch_shapes=[pltpu.VMEM((R_FOX, 1), F32), pltpu.VMEM((R_FOX, 1), F32),
                            pltpu.VMEM((R_MLA, 1), F32), pltpu.VMEM((R_MLA, 1), F32),
                            pltpu.VMEM((R_FS, MXU), F32), pltpu.VMEM((R_MLA, LANE), F32),
                            pltpu.VMEM((R_CUM, 1), F32)] + bufs + [pltpu.SemaphoreType.DMA((2, N_ARR))]),
        out_shape=[jax.ShapeDtypeStruct((s_cnt, R_FS, MXU), F32), jax.ShapeDtypeStruct((s_cnt, R_MLA, LANE), F32)],
        compiler_params=_cparams(("arbitrary", "arbitrary")),
        name="sample_attn",
    )(page_table, *qops, *new_ops, later, *caches)


def _sample_operands(r, s_cnt):
    def slots(q, nh, w, keep):
        return q.reshape(s_cnt, N_NEW, nh, w)[..., :keep].transpose(0, 2, 1, 3).reshape(s_cnt, nh * N_NEW, keep)

    def gqa(q):
        z = jnp.zeros_like(q)
        half = GROUP * N_NEW
        return jnp.concatenate([jnp.concatenate([q[:, :half], z[:, :half]], axis=2),
                                jnp.concatenate([z[:, half:], q[:, half:]], axis=2)], axis=1)

    qf = gqa(slots(r["qf"], FOX_HEADS, LANE, HEAD_DIM))
    qs = gqa(slots(r["qs"], SB_HEADS, LANE, HEAD_DIM))
    zf = jnp.zeros_like(qf)
    qfs = jnp.concatenate([jnp.concatenate([qf, zf], axis=2), jnp.concatenate([zf, qs], axis=2)], axis=1)
    qm = r["qm"].reshape(s_cnt, N_NEW, MLA_HEADS, MXU).transpose(0, 2, 1, 3).reshape(s_cnt, R_MLA, MXU)
    qm1, qm2 = qm[..., :MLA_KV_RANK], qm[..., MLA_KV_RANK:MLA_KV_RANK + MLA_ROPE]

    def pad_keys(a):
        a = a.reshape(s_cnt, N_NEW, a.shape[-1])
        return jnp.pad(a, ((0, 0), (0, PAGE - N_NEW), (0, 0)))

    nk = pad_keys(jnp.concatenate([r["fk32"], r["sk32"]], axis=1))
    nv = pad_keys(jnp.concatenate([r["fv32"], r["sv32"]], axis=1))
    nlf = r["logft"].reshape(8, s_cnt, N_NEW).transpose(1, 0, 2)
    nlf = jnp.where(jnp.arange(8)[None, :, None] < FOX_HEADS, nlf, 0.0)
    nlf = jnp.pad(nlf, ((0, 0), (0, 0), (0, PAGE - N_NEW)))
    return (qfs, qm1, qm2), (nk, nv, pad_keys(r["ckv32"]), pad_keys(r["kpe32"]), nlf)


def _sample_unpack(ofs, om, s_cnt):
    def gqa(a):
        a = a.reshape(s_cnt, FOX_KV_HEADS, GROUP, N_NEW, FOX_KV_HEADS, HEAD_DIM)
        a = jnp.stack([a[:, hk, :, :, hk, :] for hk in range(FOX_KV_HEADS)], axis=1)
        return a.transpose(0, 3, 1, 2, 4).reshape(s_cnt * N_NEW, FOX_W)

    o_fox = gqa(ofs[:, :R_FOX, :LANE])
    o_sb = gqa(ofs[:, R_FOX:, LANE:])
    o_lat = om.reshape(s_cnt, MLA_HEADS, N_NEW, MLA_KV_RANK).transpose(0, 2, 1, 3)
    return o_fox, o_lat.reshape(s_cnt * N_NEW, MLA_HEADS * MLA_KV_RANK), o_sb


def _uv_kernel(x_ref, w_ref, o_ref):
    for h in range(MLA_HEADS):
        x = x_ref[:, h * MLA_KV_RANK:(h + 1) * MLA_KV_RANK].astype(BF16)
        o_ref[:, h * MLA_V:(h + 1) * MLA_V] = _dot(x, w_ref[h])


def _uv_proj(o_lat, wuv):
    n = o_lat.shape[0]
    return pl.pallas_call(
        _uv_kernel,
        out_shape=jax.ShapeDtypeStruct((n, MLA_W), F32),
        name="uv_proj",
    )(o_lat, wuv)


def _out_kernel(of_ref, om_ref, os_ref, h_ref, gf_ref, gm_ref, gs_ref, w_ref, o_ref):
    a = _rms(of_ref[...], gf_ref[...]).astype(BF16)
    b = _rms(om_ref[...], gm_ref[...]).astype(BF16)
    c = _rms(os_ref[...], gs_ref[...]).astype(BF16)
    y = _dot(a, w_ref[0:FOX_W, :]) + _dot(b, w_ref[FOX_W:FOX_W + MLA_W, :]) + _dot(c, w_ref[FOX_W + MLA_W:, :])
    o_ref[...] = h_ref[...] + y


def _out_proj(o_fox, o_mla, o_sb, h, lw, tm):
    n = h.shape[0]

    def row(w):
        return pl.BlockSpec((tm, w), lambda i: (i, 0))

    def full(a):
        return pl.BlockSpec(a.shape, lambda i: (0, 0))

    return pl.pallas_call(
        _out_kernel,
        grid=(n // tm,),
        in_specs=[row(FOX_W), row(MLA_W), row(SB_W), row(D_MODEL), full(lw["g_of"]), full(lw["g_om"]),
                  full(lw["g_os"]), full(lw["w_out"])],
        out_specs=row(D_MODEL),
        out_shape=jax.ShapeDtypeStruct((n, D_MODEL), F32),
        compiler_params=_cparams(("arbitrary",)),
        name="out_proj",
    )(o_fox, o_mla, o_sb, h, lw["g_of"], lw["g_om"], lw["g_os"], lw["w_out"])


def _memkv_kernel(x_ref, w_ref, k_ref, v_ref):
    y = _dot(x_ref[...].astype(BF16), w_ref[0])
    k_ref[0] = y[:, :MEM_W]
    v_ref[0] = y[:, MEM_W:]


def _mem_kv(mem, w_kv, tm):
    n = mem.shape[0]
    nl = w_kv.shape[0]
    out = jax.ShapeDtypeStruct((nl, n, MEM_W), F32)
    ospec = pl.BlockSpec((1, tm, MEM_W), lambda i, l: (l, i, 0))
    return pl.pallas_call(
        _memkv_kernel,
        grid=(n // tm, nl),
        in_specs=[pl.BlockSpec((tm, D_MODEL), lambda i, l: (i, 0)),
                  pl.BlockSpec((1, D_MODEL, 2 * MEM_W), lambda i, l: (l, 0, 0))],
        out_specs=[ospec, ospec],
        out_shape=[out, out],
        compiler_params=_cparams(("arbitrary", "arbitrary")),
        name="mem_kv",
    )(mem, w_kv)


def _mem_kernel(ns, tq, h_ref, g_ref, wq_ref, k_ref, v_ref, wo_ref, o_ref):
    h = h_ref[...]
    hn = _rms(h, g_ref[...]).astype(BF16)
    q = _dot(hn, wq_ref[...]) * MEM_SCALE
    outs = []
    for hd in range(MEM_HEADS):
        sl = slice(hd * MEM_HEAD_DIM, (hd + 1) * MEM_HEAD_DIM)
        qh = q[:, sl].reshape(ns, tq, MEM_HEAD_DIM).astype(BF16)
        kh = k_ref[:, :, sl].astype(BF16)
        vh = v_ref[:, :, sl].astype(BF16)
        s = jnp.einsum("sqd,skd->sqk", qh, kh, preferred_element_type=F32)
        m = jnp.max(s, axis=-1, keepdims=True)
        p = jnp.exp(s - m)
        p = p / jnp.sum(p, axis=-1, keepdims=True)
        o = jnp.einsum("sqk,skd->sqd", p.astype(BF16), vh, preferred_element_type=F32)
        outs.append(o.reshape(ns * tq, MEM_HEAD_DIM))
    o = jnp.concatenate(outs, axis=1).astype(BF16)
    o_ref[...] = h + _dot(o, wo_ref[...])


def _mem_attn(h, mem_k, mem_v, lw, ns, tq, per_seq):
    n = h.shape[0]
    tm = ns * tq
    assert per_seq % tq == 0 and (ns == 1 or tq == per_seq)
    spb = per_seq // tq
    mt = mem_k.shape[1]

    def kv_index(i):
        return (i // spb if ns == 1 else i, 0, 0)

    def full(a):
        return pl.BlockSpec(a.shape, lambda i: (0, 0))

    return pl.pallas_call(
        functools.partial(_mem_kernel, ns, tq),
        grid=(n // tm,),
        in_specs=[pl.BlockSpec((tm, D_MODEL), lambda i: (i, 0)), full(lw["g_mem"]), full(lw["w_mq"]),
                  pl.BlockSpec((ns, mt, MEM_W), kv_index), pl.BlockSpec((ns, mt, MEM_W), kv_index),
                  full(lw["w_mo"])],
        out_specs=pl.BlockSpec((tm, D_MODEL), lambda i: (i, 0)),
        out_shape=jax.ShapeDtypeStruct((n, D_MODEL), F32),
        compiler_params=_cparams(("arbitrary",)),
        name="mem_attn",
    )(h, lw["g_mem"], lw["w_mq"], mem_k, mem_v, lw["w_mo"])


def _mlp_kernel(final, h_ref, g_ref, wu_ref, wd_ref, gfin_ref, o_ref, hn_ref, acc_ref):
    j = pl.program_id(1)

    @pl.when(j == 0)
    def _():
        hn_ref[...] = _rms(h_ref[...], g_ref[...]).astype(BF16)
        acc_ref[...] = h_ref[...]

    u = jnp.maximum(_dot(hn_ref[...], wu_ref[...]), 0.0)
    acc_ref[...] += _dot((u * u).astype(BF16), wd_ref[...])

    @pl.when(j == pl.num_programs(1) - 1)
    def _():
        y = acc_ref[...]
        o_ref[...] = _rms(y, gfin_ref[...]) if final else y


def _mlp(h, lw, g_final, final, tm, tf):
    n = h.shape[0]
    return pl.pallas_call(
        functools.partial(_mlp_kernel, final),
        grid=(n // tm, D_FF // tf),
        in_specs=[pl.BlockSpec((tm, D_MODEL), lambda i, j: (i, 0)),
                  pl.BlockSpec((1, D_MODEL), lambda i, j: (0, 0)),
                  pl.BlockSpec((D_MODEL, tf), lambda i, j: (0, j)),
                  pl.BlockSpec((tf, D_MODEL), lambda i, j: (j, 0)),
                  pl.BlockSpec((1, D_MODEL), lambda i, j: (0, 0))],
        out_specs=pl.BlockSpec((tm, D_MODEL), lambda i, j: (i, 0)),
        out_shape=jax.ShapeDtypeStruct((n, D_MODEL), F32),
        scratch_shapes=[pltpu.VMEM((tm, D_MODEL), BF16), pltpu.VMEM((tm, D_MODEL), F32)],
        compiler_params=_cparams(("arbitrary", "arbitrary")),
        name="mlp",
    )(h, lw["g_mlp"], lw["w_up"], lw["w_down"], g_final)


def _swap_halves(w, width):
    lead = w.shape[:-1]
    g = w.reshape(lead + (-1, 2, width // 2))
    return g[..., ::-1, :].reshape(w.shape)


def _lane_blocks(w, width):
    d = w.shape[0]
    g = w.reshape(d, -1, width)
    return jnp.pad(g, ((0, 0), (0, 0), (0, LANE - width))).reshape(d, -1)


def _prep_layer(l, w_in, b_f, g_mix, g_q, w_uq, g_kv, w_uk, w_uv, g_of, g_om, g_os, w_out,
                g_mem, w_mq, w_mo, g_mlp, w_up, w_down):
    sizes = (FOX_W, KV_W, KV_W, FOX_HEADS, MLA_Q_RANK, MLA_KV_RANK, MLA_ROPE, SB_W, KV_W, KV_W)
    pts, acc = [], 0
    for s in sizes[:-1]:
        acc += s
        pts.append(acc)
    fq, fk, fv, ff, mqa, mkv, mkpe, sq, sk, sv = jnp.split(w_in[l], pts, axis=1)
    d = w_in.shape[1]

    def z(w):
        return jnp.zeros((d, w), F32)

    w_in_r = jnp.concatenate(
        [fk, fv, sk, sv, mkv, mkpe, z(LANE - MLA_ROPE), _swap_halves(mkpe, MLA_ROPE), z(LANE - MLA_ROPE),
         ff, z(LANE - FOX_HEADS), mqa, z(2 * LANE - MLA_Q_RANK),
         _lane_blocks(fq, HEAD_DIM), _lane_blocks(fk, HEAD_DIM), _lane_blocks(fv, HEAD_DIM),
         _lane_blocks(sq, HEAD_DIM), _lane_blocks(sk, HEAD_DIM), _lane_blocks(sv, HEAD_DIM)], axis=1)
    assert w_in_r.shape[1] == N_INR
    uq = w_uq[l].reshape(MLA_Q_RANK, MLA_HEADS, MLA_NOPE + MLA_ROPE)
    nope = uq[:, :, :MLA_NOPE].reshape(MLA_Q_RANK, MLA_HEADS * MLA_NOPE)
    pe = uq[:, :, MLA_NOPE:].reshape(MLA_Q_RANK, MLA_HEADS * MLA_ROPE)
    w_uq_r = jnp.concatenate([nope, _lane_blocks(pe, MLA_ROPE), _lane_blocks(_swap_halves(pe, MLA_ROPE), MLA_ROPE)],
                             axis=1)
    assert w_uq_r.shape[1] == N_UQ
    w_uq_r = jnp.pad(w_uq_r, ((0, 2 * LANE - MLA_Q_RANK), (0, 0)))
    w_uk_r = jnp.pad(w_uk[l].transpose(1, 2, 0), ((0, 0), (0, 0), (0, 0)))
    return dict(
        g_mix=g_mix[l][None, :],
        w_in=w_in_r.astype(BF16),
        w_fft=jnp.pad(ff.T, ((0, 8 - FOX_HEADS), (0, 0))).astype(BF16),
        b_f=jnp.pad(b_f[l], (0, LANE - FOX_HEADS))[None, :],
        b_ft=jnp.pad(b_f[l], (0, 8 - FOX_HEADS))[:, None],
        g_q=jnp.pad(g_q[l], (0, 2 * LANE - MLA_Q_RANK))[None, :],
        w_uq=w_uq_r.astype(BF16),
        g_kv=g_kv[l][None, :],
        w_uk=w_uk_r.astype(BF16),
        w_uv=w_uv[l].transpose(1, 0, 2).astype(BF16),
        g_of=g_of[l][None, :], g_om=g_om[l][None, :], g_os=g_os[l][None, :],
        w_out=w_out[l].astype(BF16),
        g_mem=g_mem[l][None, :], w_mq=w_mq[l].astype(BF16), w_mo=w_mo[l].astype(BF16),
        g_mlp=g_mlp[l][None, :], w_up=w_up[l].astype(BF16), w_down=w_down[l].astype(BF16),
    )


def _constants():
    nblk = FOX_HEADS + 2 * FOX_KV_HEADS
    sel = np.zeros((3 * LANE, nblk * LANE), np.float32)
    ones = np.zeros((1, nblk * LANE), np.float32)
    for h in range(FOX_HEADS):
        hk, g = divmod(h, GROUP)
        for j in range(3):
            sel[j * LANE + h, h * LANE + X_C + j] = 1.0
            ones[0, h * LANE + X_G + 3 * g + j] = 1.0
            sel[j * LANE + h, (FOX_HEADS + hk) * LANE + X_G + 3 * g + j] = -1.0
    for hk in range(FOX_KV_HEADS):
        ones[0, (FOX_HEADS + hk) * LANE + X_C:(FOX_HEADS + hk) * LANE + X_C + 3] = 1.0
        ones[0, (FOX_HEADS + FOX_KV_HEADS + hk) * LANE + X_ONE_V] = 1.0
    idx = np.arange(MXU)
    later = (idx[:, None] > idx[None, :]).astype(np.float32)
    return dict(sel=jnp.asarray(sel, BF16), ones=jnp.asarray(ones, F32), later=jnp.asarray(later, BF16))


def _rope_tables(pos):
    half = MLA_ROPE // 2
    inv_freq = jnp.power(ROPE_BASE, -jnp.arange(half, dtype=F32) / half)
    ang = pos.astype(F32)[:, None] * inv_freq[None, :]
    cos, sin = jnp.cos(ang), jnp.sin(ang)
    pad = ((0, 0), (0, LANE - MLA_ROPE))
    return jnp.pad(jnp.concatenate([cos, cos], axis=1), pad), jnp.pad(jnp.concatenate([-sin, sin], axis=1), pad)


TM_IN = 512
TB_ATT = 256
TM_MLP = 1024
TF_MLP = 1024
TQ_MEM = 512
NS_MEM = 16
PP = 8


def kernel(x_prompt, x_sample, cache_fox_k, cache_fox_v, cache_fox_logf, cache_mla_ckv, cache_mla_kpe,
           cache_sb_k, cache_sb_v, cache_mem_k, cache_mem_v, page_table, mem_prompt,
           g_mix, w_in, b_fox_f, g_mla_q, w_mla_uq, g_mla_kv, w_mla_uk, w_mla_uv,
           g_out_fox, g_out_mla, g_out_sb, w_out, g_mem, w_mem_q, w_mem_k, w_mem_v, w_mem_o,
           g_mlp, w_up, w_down, g_final):
    bsz, seq, _ = x_prompt.shape
    s_cnt, n_new, _ = x_sample.shape
    depth = w_in.shape[0]
    past_len = page_table.shape[1] * PAGE
    assert n_new == N_NEW
    n_p, n_s = bsz * seq, s_cnt * n_new
    consts = _constants()

    cos_p, sin_p = _rope_tables(jnp.arange(seq, dtype=jnp.int32))
    cos_p, sin_p = jnp.tile(cos_p, (bsz, 1)), jnp.tile(sin_p, (bsz, 1))
    cos_s, sin_s = _rope_tables(past_len + jnp.arange(n_new, dtype=jnp.int32))
    cos_s, sin_s = jnp.tile(cos_s, (s_cnt, 1)), jnp.tile(sin_s, (s_cnt, 1))

    pool = cache_fox_k.shape[1]

    def keys_minor(c):
        return c.transpose(0, 1, 3, 4, 2).reshape(depth, pool, KV_W, PAGE)

    caches = (keys_minor(cache_fox_k), keys_minor(cache_fox_v), cache_fox_logf.transpose(0, 3, 1, 2),
              cache_mla_ckv, cache_mla_kpe.transpose(0, 1, 3, 2), keys_minor(cache_sb_k), keys_minor(cache_sb_v))
    mem_tok = mem_prompt.shape[1]
    w_kv = jnp.concatenate([w_mem_k, w_mem_v], axis=2).astype(BF16)
    p_mem_k, p_mem_v = _mem_kv(mem_prompt.reshape(bsz * mem_tok, D_MODEL), w_kv, 512)
    s_mem_k = cache_mem_k.reshape(depth, s_cnt, mem_tok, MEM_W)
    s_mem_v = cache_mem_v.reshape(depth, s_cnt, mem_tok, MEM_W)
    g_fin = g_final[None, :]

    hp = x_prompt.reshape(n_p, D_MODEL)
    hs = x_sample.reshape(n_s, D_MODEL)
    rows_p, rows_s = [], []
    tm_p, tm_s = min(TM_IN, seq), min(TM_IN, n_s)
    tmlp_p, tmlp_s = min(TM_MLP, n_p), min(TM_MLP, n_s)
    tb = min(TB_ATT, seq)
    tq_mem = min(TQ_MEM, seq)
    ns_mem = min(NS_MEM, s_cnt)
    pp = min(PP, page_table.shape[1])
    for l in range(depth):
        lw = _prep_layer(l, w_in, b_fox_f, g_mix, g_mla_q, w_mla_uq, g_mla_kv, w_mla_uk, w_mla_uv,
                         g_out_fox, g_out_mla, g_out_sb, w_out, g_mem, w_mem_q, w_mem_o, g_mlp, w_up, w_down)
        final = l == depth - 1

        r = _in_proj(hp, lw, consts, cos_p, sin_p, seq, tm_p)
        o_fox, o_mla, o_sb = _prompt_attn(bsz, seq, tb, r, consts["later"], lw["w_uv"])
        hp = _out_proj(o_fox, o_mla, o_sb, hp, lw, tm_p)
        hp = _mem_attn(hp, p_mem_k[l].reshape(bsz, mem_tok, MEM_W), p_mem_v[l].reshape(bsz, mem_tok, MEM_W),
                       lw, 1, tq_mem, seq)
        hp = _mlp(hp, lw, g_fin, final, tmlp_p, TF_MLP)
        rows_p.append(r)

        r = _in_proj(hs, lw, consts, cos_s, sin_s, n_new, tm_s)
        qops, new_ops = _sample_operands(r, s_cnt)
        ofs, om = _sample_attn(l, page_table, qops, new_ops, consts["later"], caches, pp)
        o_fox, o_lat, o_sb = _sample_unpack(ofs, om, s_cnt)
        o_mla = _uv_proj(o_lat, lw["w_uv"])
        hs = _out_proj(o_fox, o_mla, o_sb, hs, lw, tm_s)
        hs = _mem_attn(hs, s_mem_k[l], s_mem_v[l], lw, ns_mem, n_new, n_new)
        hs = _mlp(hs, lw, g_fin, final, tmlp_s, TF_MLP)
        rows_s.append(r)

    def stack(rows, key, shape):
        return jnp.stack([r[key] for r in rows]).reshape((depth,) + shape)

    def group_rows(rows, lead):
        return (stack(rows, "fk32", lead + (FOX_KV_HEADS, HEAD_DIM)), stack(rows, "fv32", lead + (FOX_KV_HEADS, HEAD_DIM)),
                stack(rows, "logf", lead + (FOX_HEADS,)), stack(rows, "ckv32", lead + (MLA_KV_RANK,)),
                stack(rows, "kpe32", lead + (MLA_ROPE,)),
                stack(rows, "sk32", lead + (SB_KV_HEADS, HEAD_DIM)), stack(rows, "sv32", lead + (SB_KV_HEADS, HEAD_DIM)))

    y_prompt = hp.reshape(bsz, seq, D_MODEL)
    y_sample = hs.reshape(s_cnt, n_new, D_MODEL)
    pm_shape = (depth, bsz, mem_tok, MEM_HEADS, MEM_HEAD_DIM)
    return ((y_prompt, y_sample) + group_rows(rows_p, (bsz, seq))
            + (p_mem_k.reshape(pm_shape), p_mem_v.reshape(pm_shape)) + group_rows(rows_s, (s_cnt, n_new)))
```

```python
import functools
import math

import jax
import jax.numpy as jnp
import numpy as np
from jax import lax
from jax.experimental import pallas as pl
from jax.experimental.pallas import tpu as pltpu

F32 = jnp.float32
BF16 = jnp.bfloat16

D_MODEL = 1024
HEAD_DIM = 64
FOX_HEADS = 6
FOX_KV_HEADS = 2
SB_HEADS = 6
SB_KV_HEADS = 2
GROUP = 3
MLA_HEADS = 4
MLA_Q_RANK = 192
MLA_KV_RANK = 128
MLA_NOPE = 64
MLA_ROPE = 32
MLA_V = 64
ROPE_BASE = 10000.0
MEM_HEADS = 4
MEM_HEAD_DIM = 64
MEM_W = MEM_HEADS * MEM_HEAD_DIM
D_FF = 4 * D_MODEL
EPS = 1e-6
PAGE = 128
N_NEW = 8
FOX_W = FOX_HEADS * HEAD_DIM
MLA_W = MLA_HEADS * MLA_V
SB_W = SB_HEADS * HEAD_DIM
KV_W = FOX_KV_HEADS * HEAD_DIM
ATT_SCALE = 1.0 / math.sqrt(HEAD_DIM)
MLA_SCALE = 1.0 / math.sqrt(MLA_NOPE + MLA_ROPE)
MEM_SCALE = 1.0 / math.sqrt(MEM_HEAD_DIM)
LOG2E = math.log2(math.e)
MLA_EXP2 = MLA_SCALE * LOG2E

LANE = 128
MXU = 256
VMEM_LIMIT = 48 * 1024 * 1024

X_C = HEAD_DIM
X_G = HEAD_DIM + 3
X_ONE_V = HEAD_DIM
X_ONE_M = MLA_KV_RANK + MLA_ROPE

C_FK = 0
C_FV = C_FK + KV_W
C_SK = C_FV + KV_W
C_SV = C_SK + KV_W
C_CKV = C_SV + KV_W
C_KPE = C_CKV + MLA_KV_RANK
C_KPS = C_KPE + LANE
C_FF = C_KPS + LANE
C_MQA = C_FF + LANE
C_PQF = C_MQA + 2 * LANE
C_PKF = C_PQF + FOX_HEADS * LANE
C_PVF = C_PKF + FOX_KV_HEADS * LANE
C_PQS = C_PVF + FOX_KV_HEADS * LANE
C_PKS = C_PQS + SB_HEADS * LANE
C_PVS = C_PKS + SB_KV_HEADS * LANE
N_INR = C_PVS + SB_KV_HEADS * LANE

U_NOPE = 0
U_PE = MLA_HEADS * MLA_NOPE
U_PS = U_PE + MLA_HEADS * LANE
N_UQ = U_PS + MLA_HEADS * LANE

NEG_INF = float("-inf")


def _cparams(sem):
    return pltpu.CompilerParams(dimension_semantics=sem, vmem_limit_bytes=VMEM_LIMIT)


def _log_sigmoid(x):
    return jnp.minimum(x, 0.0) - jnp.log(1.0 + jnp.exp(-jnp.abs(x)))


def _split3(x):
    hi = x.astype(BF16)
    r = x - hi.astype(F32)
    mid = r.astype(BF16)
    lo = (r - mid.astype(F32)).astype(BF16)
    return hi, mid, lo


def _split2(x):
    hi = x.astype(BF16)
    lo = (x - hi.astype(F32)).astype(BF16)
    return hi, lo


def _dot(a, b):
    return jnp.dot(a, b, preferred_element_type=F32)


def _dot_nt(a, b):
    return lax.dot_general(a, b, (((1,), (1,)), ((), ())), preferred_element_type=F32)


def _rms(x, g, n=None):
    n = x.shape[-1] if n is None else n
    ms = jnp.sum(x * x, axis=-1, keepdims=True) * (1.0 / n)
    return x * lax.rsqrt(ms + EPS) * g


def _in_kernel(seg, tm, h_ref, g_ref, w_ref, wfft_ref, bf_ref, bft_ref, gq_ref, wuq_ref, gkv_ref,
               wuk_ref, cos_ref, sin_ref, sel_ref, one_ref,
               fk32, fv32, sk32, sv32, ckv32, kpe32, logf_o, logft_o,
               qf_o, kf_o, vf_o, qs_o, ks_o, vs_o, qm_o, kvm_o, carry_ref):
    i = pl.program_id(0)
    xn = _rms(h_ref[...], g_ref[...]).astype(BF16)
    proj = _dot(xn, w_ref[...])

    fk32[...] = proj[:, C_FK:C_FK + KV_W]
    fv32[...] = proj[:, C_FV:C_FV + KV_W]
    sk32[...] = proj[:, C_SK:C_SK + KV_W]
    sv32[...] = proj[:, C_SV:C_SV + KV_W]

    logf = _log_sigmoid(proj[:, C_FF:C_FF + LANE] + bf_ref[...])
    logf_o[...] = logf[:, :FOX_HEADS]
    logft_o[...] = _log_sigmoid(_dot_nt(wfft_ref[...], xn) + bft_ref[...])
    r = lax.broadcasted_iota(jnp.int32, (tm, tm), 0)
    cidx = lax.broadcasted_iota(jnp.int32, (tm, tm), 1)
    lower = cidx <= r
    if seg < tm:
        lower = lower & ((r // seg) == (cidx // seg))
    lower = jnp.where(lower, 1.0, 0.0).astype(BF16)
    hi, mid, lo = _split3(logf)
    c3 = _dot(lower, jnp.concatenate([hi, mid, lo], axis=1))
    c = (c3[:, :LANE] + c3[:, LANE:2 * LANE]) + c3[:, 2 * LANE:]
    if seg > tm:
        @pl.when(i % (seg // tm) == 0)
        def _():
            carry_ref[...] = jnp.zeros_like(carry_ref)

        c = c + carry_ref[...]
        carry_ref[...] = c[tm - 1:tm, :]
    hi, mid, lo = _split3(c)
    ext = _dot(jnp.concatenate([hi, mid, lo], axis=1), sel_ref[...]) + one_ref[...]
    nq = FOX_HEADS * LANE
    nk = FOX_KV_HEADS * LANE
    qf_o[...] = (proj[:, C_PQF:C_PQF + nq] * ATT_SCALE + ext[:, :nq]).astype(BF16)
    kf_o[...] = (proj[:, C_PKF:C_PKF + nk] + ext[:, nq:nq + nk]).astype(BF16)
    vf_o[...] = (proj[:, C_PVF:C_PVF + nk] + ext[:, nq + nk:]).astype(BF16)
    qs_o[...] = (proj[:, C_PQS:C_PQS + nq] * ATT_SCALE).astype(BF16)
    ks_o[...] = proj[:, C_PKS:C_PKS + nk].astype(BF16)
    vs_o[...] = proj[:, C_PVS:C_PVS + nk].astype(BF16)

    ckv = _rms(proj[:, C_CKV:C_CKV + MLA_KV_RANK], gkv_ref[...])
    ckv32[...] = ckv
    cos = cos_ref[...]
    sin = sin_ref[...]
    kpe = proj[:, C_KPE:C_KPE + LANE] * cos + proj[:, C_KPS:C_KPS + LANE] * sin
    kpe32[...] = kpe[:, :MLA_ROPE]
    lane = lax.broadcasted_iota(jnp.int32, (1, LANE), 1)
    kvm_o[:, :LANE] = ckv.astype(BF16)
    kvm_o[:, LANE:] = (kpe + jnp.where(lane == MLA_ROPE, 1.0, 0.0)).astype(BF16)
    qa = _rms(proj[:, C_MQA:C_MQA + 2 * LANE], gq_ref[...], MLA_Q_RANK).astype(BF16)
    q = _dot(qa, wuq_ref[...])
    for hh in range(MLA_HEADS):
        qn = q[:, U_NOPE + hh * MLA_NOPE:U_NOPE + (hh + 1) * MLA_NOPE].astype(BF16)
        qm_o[:, hh * MXU:hh * MXU + LANE] = _dot(qn, wuk_ref[hh]).astype(BF16)
        qpe = (q[:, U_PE + hh * LANE:U_PE + (hh + 1) * LANE] * cos
               + q[:, U_PS + hh * LANE:U_PS + (hh + 1) * LANE] * sin)
        qm_o[:, hh * MXU + LANE:(hh + 1) * MXU] = qpe.astype(BF16)


def _in_proj(h, lw, consts, cos, sin, seg, tm):
    n = h.shape[0]
    assert n % tm == 0 and (seg % tm == 0 or tm % seg == 0)

    def row(w):
        return pl.BlockSpec((tm, w), lambda i: (i, 0))

    def full(a):
        nd = a.ndim
        return pl.BlockSpec(a.shape, lambda i, _nd=nd: (0,) * _nd)

    ins = [h, lw["g_mix"], lw["w_in"], lw["w_fft"], lw["b_f"], lw["b_ft"], lw["g_q"], lw["w_uq"],
           lw["g_kv"], lw["w_uk"], cos, sin, consts["sel"], consts["ones"]]
    in_specs = [row(D_MODEL)] + [full(a) for a in ins[1:10]] + [row(LANE), row(LANE), full(ins[12]), full(ins[13])]
    outs = [
        ("fk32", KV_W, F32), ("fv32", KV_W, F32), ("sk32", KV_W, F32), ("sv32", KV_W, F32),
        ("ckv32", MLA_KV_RANK, F32), ("kpe32", MLA_ROPE, F32), ("logf", FOX_HEADS, F32), ("logft", None, F32),
        ("qf", FOX_HEADS * LANE, BF16), ("kf", FOX_KV_HEADS * LANE, BF16), ("vf", FOX_KV_HEADS * LANE, BF16),
        ("qs", SB_HEADS * LANE, BF16), ("ks", SB_KV_HEADS * LANE, BF16), ("vs", SB_KV_HEADS * LANE, BF16),
        ("qm", MLA_HEADS * MXU, BF16), ("kvm", MXU, BF16),
    ]
    out_shape, out_specs = [], []
    for _, w, dt in outs:
        if w is None:
            out_shape.append(jax.ShapeDtypeStruct((8, n), dt))
            out_specs.append(pl.BlockSpec((8, tm), lambda i: (0, i)))
        else:
            out_shape.append(jax.ShapeDtypeStruct((n, w), dt))
            out_specs.append(row(w))
    res = pl.pallas_call(
        functools.partial(_in_kernel, seg, tm),
        grid=(n // tm,),
        in_specs=in_specs,
        out_specs=out_specs,
        out_shape=out_shape,
        scratch_shapes=[pltpu.VMEM((1, LANE), F32)],
        compiler_params=_cparams(("arbitrary",)),
        name="in_proj",
    )(*ins)
    return dict(zip([o[0] for o in outs], res))


def _pair_tables(nq):
    qi, kj = [], []
    for i in range(nq):
        for j in range(i, -1, -1):
            qi.append(i)
            kj.append(j)
    return jnp.asarray(qi, jnp.int32), jnp.asarray(kj, jnp.int32)


def _heads_on_rows(ref, first, count, width):
    return jnp.concatenate([ref[:, (first + g) * width:(first + g + 1) * width] for g in range(count)], axis=0)


def _attn_kernel(tb, qi_ref, kj_ref, qf_ref, kf_ref, vf_ref, qs_ref, ks_ref, vs_ref, qm_ref, kvm_ref,
                 later_ref, wuv_ref, of_ref, om_ref, os_ref,
                 fm_ref, facc_ref, mm_ref, macc_ref, sacc_ref, scar_ref):
    p = pl.program_id(1)
    qi = qi_ref[p]
    kj = kj_ref[p]

    @pl.when(kj == qi)
    def _():
        fm_ref[...] = jnp.full_like(fm_ref, NEG_INF)
        mm_ref[...] = jnp.full_like(mm_ref, NEG_INF)
        facc_ref[...] = jnp.zeros_like(facc_ref)
        macc_ref[...] = jnp.zeros_like(macc_ref)
        sacc_ref[...] = jnp.zeros_like(sacc_ref)
        scar_ref[...] = jnp.zeros_like(scar_ref)

    def body(diag):
        def masks(rows):
            r = lax.broadcasted_iota(jnp.int32, (rows, tb), 0) % tb
            c = lax.broadcasted_iota(jnp.int32, (rows, tb), 1)
            return c <= r, c < r

        if diag:
            allow_g, strict_g = masks(GROUP * tb)
            allow_m, _ = masks(MLA_HEADS * tb)

        for hk in range(FOX_KV_HEADS):
            q = _heads_on_rows(qf_ref, hk * GROUP, GROUP, LANE)
            s = _dot_nt(q, kf_ref[:, hk * LANE:(hk + 1) * LANE])
            if diag:
                s = jnp.where(allow_g, s, NEG_INF)
            m_prev = fm_ref[hk]
            m_new = jnp.maximum(m_prev, jnp.max(s, axis=1, keepdims=True))
            pm = jnp.exp(s - m_new).astype(BF16)
            facc_ref[hk] = jnp.exp(m_prev - m_new) * facc_ref[hk] + _dot(pm, vf_ref[:, hk * LANE:(hk + 1) * LANE])
            fm_ref[hk] = m_new

        q = _heads_on_rows(qm_ref, 0, MLA_HEADS, MXU)
        kv = kvm_ref[...]
        s = _dot_nt(q, kv)
        if diag:
            s = jnp.where(allow_m, s, NEG_INF)
        m_prev = mm_ref[...]
        m_new = jnp.maximum(m_prev, jnp.max(s, axis=1, keepdims=True))
        pm = jnp.exp2((s - m_new) * MLA_EXP2).astype(BF16)
        macc_ref[...] = jnp.exp2((m_prev - m_new) * MLA_EXP2) * macc_ref[...] + _dot(pm, kv)
        mm_ref[...] = m_new

        later = later_ref[...]
        rows = GROUP * tb
        for hk in range(SB_KV_HEADS):
            q = _heads_on_rows(qs_ref, hk * GROUP, GROUP, LANE)
            z = _dot_nt(q, ks_ref[:, hk * LANE:(hk + 1) * LANE]) * LOG2E
            ls = jnp.minimum(z, 0.0) - jnp.log2(1.0 + jnp.exp2(-jnp.abs(z)))
            lk = ls - z
            if diag:
                lk = jnp.where(strict_g, lk, 0.0)
            hi, lo = _split2(lk)
            hl = jnp.concatenate([hi, lo], axis=0)
            car = scar_ref[hk]
            parts = [None] * (tb // MXU)
            for t in range(tb // MXU - 1, -1, -1):
                sf = _dot(hl[:, t * MXU:(t + 1) * MXU], later)
                sf = sf[:rows] + sf[rows:]
                parts[t] = sf + car
                car = car + (sf[:, 0:1] + lk[:, t * MXU:t * MXU + 1])
            scar_ref[hk] = car
            sfx = parts[0] if len(parts) == 1 else jnp.concatenate(parts, axis=1)
            a = jnp.exp2(ls + sfx)
            if diag:
                a = jnp.where(strict_g, a, 0.0)
            sacc_ref[hk] = sacc_ref[hk] + _dot(a.astype(BF16), vs_ref[:, hk * LANE:(hk + 1) * LANE])

    @pl.when(kj == qi)
    def _():
        body(True)

    @pl.when(kj < qi)
    def _():
        body(False)

    @pl.when(kj == 0)
    def _():
        for hk in range(FOX_KV_HEADS):
            for g in range(GROUP):
                h = hk * GROUP + g
                rows = slice(g * tb, (g + 1) * tb)
                of_ref[:, h * HEAD_DIM:(h + 1) * HEAD_DIM] = (
                    facc_ref[hk, rows, 0:HEAD_DIM] / facc_ref[hk, rows, X_ONE_V:X_ONE_V + 1])
                os_ref[:, h * HEAD_DIM:(h + 1) * HEAD_DIM] = sacc_ref[hk, rows, 0:HEAD_DIM]
        for h in range(MLA_HEADS):
            rows = slice(h * tb, (h + 1) * tb)
            o_lat = (macc_ref[rows, 0:MLA_KV_RANK] / macc_ref[rows, X_ONE_M:X_ONE_M + 1]).astype(BF16)
            om_ref[:, h * MLA_V:(h + 1) * MLA_V] = _dot(o_lat, wuv_ref[h])


def _prompt_attn(bsz, seq, tb, r, later, wuv):
    nq = seq // tb
    qi_t, kj_t = _pair_tables(nq)
    n = bsz * seq

    def qspec(w):
        return pl.BlockSpec((tb, w), lambda b, p, qi, kj: (b * nq + qi[p], 0))

    def kspec(w):
        return pl.BlockSpec((tb, w), lambda b, p, qi, kj: (b * nq + kj[p], 0))

    nq_l, nk_l = FOX_HEADS * LANE, FOX_KV_HEADS * LANE
    in_specs = [qspec(nq_l), kspec(nk_l), kspec(nk_l), qspec(nq_l), kspec(nk_l), kspec(nk_l),
                qspec(MLA_HEADS * MXU), kspec(MXU),
                pl.BlockSpec(later.shape, lambda b, p, qi, kj: (0, 0)),
                pl.BlockSpec(wuv.shape, lambda b, p, qi, kj: (0, 0, 0))]
    scratch = [pltpu.VMEM((FOX_KV_HEADS, GROUP * tb, 1), F32), pltpu.VMEM((FOX_KV_HEADS, GROUP * tb, LANE), F32),
               pltpu.VMEM((MLA_HEADS * tb, 1), F32), pltpu.VMEM((MLA_HEADS * tb, MXU), F32),
               pltpu.VMEM((SB_KV_HEADS, GROUP * tb, LANE), F32), pltpu.VMEM((SB_KV_HEADS, GROUP * tb, 1), F32)]
    return pl.pallas_call(
        functools.partial(_attn_kernel, tb),
        grid_spec=pltpu.PrefetchScalarGridSpec(
            num_scalar_prefetch=2,
            grid=(bsz, int(qi_t.shape[0])),
            in_specs=in_specs,
            out_specs=[qspec(FOX_W), qspec(MLA_W), qspec(SB_W)],
            scratch_shapes=scratch),
        out_shape=[jax.ShapeDtypeStruct((n, FOX_W), F32), jax.ShapeDtypeStruct((n, MLA_W), F32),
                   jax.ShapeDtypeStruct((n, SB_W), F32)],
        compiler_params=_cparams(("arbitrary", "arbitrary")),
        name="prompt_attn",
    )(qi_t, kj_t, r["qf"], r["kf"], r["vf"], r["qs"], r["ks"], r["vs"], r["qm"], r["kvm"], later, wuv)


R_FOX = FOX_HEADS * N_NEW
R_FS = R_FOX + SB_HEADS * N_NEW
R_MLA = MLA_HEADS * N_NEW
R_CUM = SB_HEADS * N_NEW + 8
N_ARR = 7


def _sample_kernel(pp, nch, layer, pt_ref, qfs_ref, qm1_ref, qm2_ref, nk_ref, nv_ref, nckv_ref, nkpe_ref, nlf_ref,
                   later_ref, *rest):
    caches = rest[:N_ARR]
    ofs_ref, om_ref = rest[N_ARR:N_ARR + 2]
    mf_ref, lf_ref, mm_ref, lm_ref, accfs_ref, accm_ref, carry_ref = rest[N_ARR + 2:N_ARR + 9]
    bufs = rest[N_ARR + 9:2 * N_ARR + 9]
    sem = rest[2 * N_ARR + 9]
    b = pl.program_id(0)
    cstep = pl.program_id(1)
    step = b * nch + cstep
    slot = step % 2
    qfs = qfs_ref[0]
    qm1 = qm1_ref[0]
    qm2 = qm2_ref[0]

    def page_copies(bb, cc, sl):
        out = []
        for i in range(pp):
            pg = pt_ref[bb, (nch - 1 - cc) * pp + i]
            for a in range(N_ARR):
                if a == 2:
                    src = caches[a].at[layer, :, pl.ds(pl.multiple_of((pg // 8) * 8, 8), 8), :]
                else:
                    src = caches[a].at[layer, pg]
                out.append(pltpu.make_async_copy(src, bufs[a].at[sl, i], sem.at[sl, a]))
        return out

    @pl.when(step == 0)
    def _():
        for cp in page_copies(b, cstep, slot):
            cp.start()

    def block(s_fs, s_m, lf8, width, new, pv_fs, pv_m):
        z = s_fs[R_FOX:]
        ls = jnp.minimum(z, 0.0) - jnp.log(1.0 + jnp.exp(-jnp.abs(z)))
        lk = ls - z
        if new:
            key = lax.broadcasted_iota(jnp.int32, (R_FOX, width), 1)
            tok = lax.broadcasted_iota(jnp.int32, (R_FOX, width), 0) % N_NEW
            strict = key < tok
            allow = key <= tok
            keym = lax.broadcasted_iota(jnp.int32, (R_MLA, width), 1)
            allow_m = keym <= (lax.broadcasted_iota(jnp.int32, (R_MLA, width), 0) % N_NEW)
            lk = jnp.where(strict, lk, 0.0)
        y = jnp.concatenate([lk, lf8], axis=0)
        hi, lo = _split2(y)
        hl = jnp.concatenate([hi, lo], axis=0)
        tile = min(width, MXU)
        nt = width // tile
        later = later_ref[0:tile, 0:tile]
        sfs, tots = [], []
        for t in range(nt):
            sf = _dot(hl[:, t * tile:(t + 1) * tile], later)
            sf = sf[:R_CUM] + sf[R_CUM:]
            sfs.append(sf)
            tots.append(sf[:, 0:1] + y[:, t * tile:t * tile + 1])
        car = carry_ref[...]
        parts = [None] * nt
        for t in range(nt - 1, -1, -1):
            parts[t] = sfs[t] + car
            car = car + tots[t]
        carry_ref[...] = car
        sfx = parts[0] if nt == 1 else jnp.concatenate(parts, axis=1)
        decay = jnp.concatenate(
            [jnp.broadcast_to(sfx[R_CUM - 8 + h:R_CUM - 7 + h], (N_NEW, width)) for h in range(FOX_HEADS)], axis=0)
        tf = s_fs[:R_FOX] + decay
        a = jnp.exp(ls + sfx[:R_CUM - 8])
        if new:
            tf = jnp.where(allow, tf, NEG_INF)
            s_m = jnp.where(allow_m, s_m, NEG_INF)
            a = jnp.where(strict, a, 0.0)
        m_prev = mf_ref[...]
        m_new = jnp.maximum(m_prev, jnp.max(tf, axis=1, keepdims=True))
        alpha = jnp.exp(m_prev - m_new)
        pf = jnp.exp(tf - m_new)
        lf_ref[...] = alpha * lf_ref[...] + jnp.sum(pf, axis=1, keepdims=True)
        mf_ref[...] = m_new
        pfs = jnp.concatenate([pf, a], axis=0).astype(BF16)
        o = pv_fs(pfs)
        accfs_ref[0:R_FOX, :] = alpha * accfs_ref[0:R_FOX, :] + o[:R_FOX]
        accfs_ref[R_FOX:, :] = accfs_ref[R_FOX:, :] + o[R_FOX:]
        m_prev = mm_ref[...]
        m_new = jnp.maximum(m_prev, jnp.max(s_m, axis=1, keepdims=True))
        alpha = jnp.exp2((m_prev - m_new) * MLA_EXP2)
        pm = jnp.exp2((s_m - m_new) * MLA_EXP2)
        lm_ref[...] = alpha * lm_ref[...] + jnp.sum(pm, axis=1, keepdims=True)
        mm_ref[...] = m_new
        accm_ref[...] = alpha * accm_ref[...] + pv_m(pm.astype(BF16))

    @pl.when(cstep == 0)
    def _():
        mf_ref[...] = jnp.full_like(mf_ref, NEG_INF)
        mm_ref[...] = jnp.full_like(mm_ref, NEG_INF)
        lf_ref[...] = jnp.zeros_like(lf_ref)
        lm_ref[...] = jnp.zeros_like(lm_ref)
        accfs_ref[...] = jnp.zeros_like(accfs_ref)
        accm_ref[...] = jnp.zeros_like(accm_ref)
        carry_ref[...] = jnp.zeros_like(carry_ref)
        nk = nk_ref[0].astype(BF16)
        nv = nv_ref[0].astype(BF16)
        nckv = nckv_ref[0].astype(BF16)
        nkpe = nkpe_ref[0].astype(BF16)
        block(_dot_nt(qfs, nk), _dot_nt(qm1, nckv) + _dot_nt(qm2, nkpe), nlf_ref[0], PAGE, True,
              lambda pfs: _dot(pfs, nv), lambda pm: _dot(pm, nckv))

    for cp in page_copies(b, cstep, slot):
        cp.wait()

    last_c = cstep == nch - 1
    last_step = step == pl.num_programs(0) * nch - 1
    nb = jnp.where(last_c & jnp.logical_not(last_step), b + 1, b)
    nc = jnp.where(last_step, cstep, jnp.where(last_c, 0, cstep + 1))
    for cp in page_copies(nb, nc, 1 - slot):
        cp.start()

    def cat(a, axis):
        return jnp.concatenate([bufs[a][slot, i].astype(BF16) for i in range(pp)], axis=axis)

    kfs = jnp.concatenate([cat(0, 1), cat(5, 1)], axis=0)
    vfs = jnp.concatenate([cat(1, 1), cat(6, 1)], axis=0)
    ckv = cat(3, 0)
    kpet = cat(4, 1)
    lfs = []
    for i in range(pp):
        row = pt_ref[b, (nch - 1 - cstep) * pp + i] % 8
        six = jnp.concatenate([bufs[2][slot, i, h, pl.ds(row, 1), :] for h in range(FOX_HEADS)], axis=0)
        lfs.append(jnp.concatenate([six, jnp.zeros((8 - FOX_HEADS, PAGE), F32)], axis=0))
    lf8 = jnp.concatenate(lfs, axis=1)
    block(_dot(qfs, kfs), _dot_nt(qm1, ckv) + _dot(qm2, kpet), lf8, pp * PAGE, False,
          lambda pfs: _dot_nt(pfs, vfs), lambda pm: _dot(pm, ckv))

    @pl.when(cstep == nch - 1)
    def _():
        inv = 1.0 / lf_ref[...]
        ofs_ref[0, 0:R_FOX, :] = accfs_ref[0:R_FOX, :] * inv
        ofs_ref[0, R_FOX:, :] = accfs_ref[R_FOX:, :]
        om_ref[0] = accm_ref[...] * (1.0 / lm_ref[...])

    @pl.when(last_step)
    def _():
        for cp in page_copies(nb, nc, 1 - slot):
            cp.wait()


def _sample_attn(layer, page_table, qops, new_ops, later, caches, pp):
    s_cnt, n_pages = page_table.shape
    assert n_pages % pp == 0
    nch = n_pages // pp

    def per_sample(a):
        return pl.BlockSpec((1,) + a.shape[1:], lambda b, c, pt: (b, 0, 0))

    in_specs = [per_sample(a) for a in qops] + [per_sample(a) for a in new_ops]
    in_specs.append(pl.BlockSpec(later.shape, lambda b, c, pt: (0, 0)))
    in_specs += [pl.BlockSpec(memory_space=pl.ANY)] * N_ARR
    bufs = [pltpu.VMEM((2, pp, FOX_HEADS, 8, PAGE) if k == 2 else (2, pp) + a.shape[2:], F32)
            for k, a in enumerate(caches)]
    return pl.pallas_call(
        functools.partial(_sample_kernel, pp, nch, layer),
        grid_spec=pltpu.PrefetchScalarGridSpec(
            num_scalar_prefetch=1,
            grid=(s_cnt, nch),
            in_specs=in_specs,
            out_specs=[pl.BlockSpec((1, R_FS, MXU), lambda b, c, pt: (b, 0, 0)),
                       pl.BlockSpec((1, R_MLA, LANE), lambda b, c, pt: (b, 0, 0))],
            scratch_shapes=[pltpu.VMEM((R_FOX, 1), F32), pltpu.VMEM((R_FOX, 1), F32),
                            pltpu.VMEM((R_MLA, 1), F32), pltpu.VMEM((R_MLA, 1), F32),
                            pltpu.VMEM((R_FS, MXU), F32), pltpu.VMEM((R_MLA, LANE), F32),
                            pltpu.VMEM((R_CUM, 1), F32)] + bufs + [pltpu.SemaphoreType.DMA((2, N_ARR))]),
        out_shape=[jax.ShapeDtypeStruct((s_cnt, R_FS, MXU), F32), jax.ShapeDtypeStruct((s_cnt, R_MLA, LANE), F32)],
        compiler_params=_cparams(("arbitrary", "arbitrary")),
        name="sample_attn",
    )(page_table, *qops, *new_ops, later, *caches)


def _sample_operands(r, s_cnt):
    def slots(q, nh, w, keep):
        return q.reshape(s_cnt, N_NEW, nh, w)[..., :keep].transpose(0, 2, 1, 3).reshape(s_cnt, nh * N_NEW, keep)

    def gqa(q):
        z = jnp.zeros_like(q)
        half = GROUP * N_NEW
        return jnp.concatenate([jnp.concatenate([q[:, :half], z[:, :half]], axis=2),
                                jnp.concatenate([z[:, half:], q[:, half:]], axis=2)], axis=1)

    qf = gqa(slots(r["qf"], FOX_HEADS, LANE, HEAD_DIM))
    qs = gqa(slots(r["qs"], SB_HEADS, LANE, HEAD_DIM))
    zf = jnp.zeros_like(qf)
    qfs = jnp.concatenate([jnp.concatenate([qf, zf], axis=2), jnp.concatenate([zf, qs], axis=2)], axis=1)
    qm = r["qm"].reshape(s_cnt, N_NEW, MLA_HEADS, MXU).transpose(0, 2, 1, 3).reshape(s_cnt, R_MLA, MXU)
    qm1, qm2 = qm[..., :MLA_KV_RANK], qm[..., MLA_KV_RANK:MLA_KV_RANK + MLA_ROPE]

    def pad_keys(a):
        a = a.reshape(s_cnt, N_NEW, a.shape[-1])
        return jnp.pad(a, ((0, 0), (0, PAGE - N_NEW), (0, 0)))

    nk = pad_keys(jnp.concatenate([r["fk32"], r["sk32"]], axis=1))
    nv = pad_keys(jnp.concatenate([r["fv32"], r["sv32"]], axis=1))
    nlf = r["logft"].reshape(8, s_cnt, N_NEW).transpose(1, 0, 2)
    nlf = jnp.where(jnp.arange(8)[None, :, None] < FOX_HEADS, nlf, 0.0)
    nlf = jnp.pad(nlf, ((0, 0), (0, 0), (0, PAGE - N_NEW)))
    return (qfs, qm1, qm2), (nk, nv, pad_keys(r["ckv32"]), pad_keys(r["kpe32"]), nlf)


def _sample_unpack(ofs, om, s_cnt):
    def gqa(a):
        a = a.reshape(s_cnt, FOX_KV_HEADS, GROUP, N_NEW, FOX_KV_HEADS, HEAD_DIM)
        a = jnp.stack([a[:, hk, :, :, hk, :] for hk in range(FOX_KV_HEADS)], axis=1)
        return a.transpose(0, 3, 1, 2, 4).reshape(s_cnt * N_NEW, FOX_W)

    o_fox = gqa(ofs[:, :R_FOX, :LANE])
    o_sb = gqa(ofs[:, R_FOX:, LANE:])
    o_lat = om.reshape(s_cnt, MLA_HEADS, N_NEW, MLA_KV_RANK).transpose(0, 2, 1, 3)
    return o_fox, o_lat.reshape(s_cnt * N_NEW, MLA_HEADS * MLA_KV_RANK), o_sb


def _uv_kernel(x_ref, w_ref, o_ref):
    for h in range(MLA_HEADS):
        x = x_ref[:, h * MLA_KV_RANK:(h + 1) * MLA_KV_RANK].astype(BF16)
        o_ref[:, h * MLA_V:(h + 1) * MLA_V] = _dot(x, w_ref[h])


def _uv_proj(o_lat, wuv):
    n = o_lat.shape[0]
    return pl.pallas_call(
        _uv_kernel,
        out_shape=jax.ShapeDtypeStruct((n, MLA_W), F32),
        name="uv_proj",
    )(o_lat, wuv)


def _out_kernel(of_ref, om_ref, os_ref, h_ref, gf_ref, gm_ref, gs_ref, w_ref, o_ref):
    a = _rms(of_ref[...], gf_ref[...]).astype(BF16)
    b = _rms(om_ref[...], gm_ref[...]).astype(BF16)
    c = _rms(os_ref[...], gs_ref[...]).astype(BF16)
    y = _dot(a, w_ref[0:FOX_W, :]) + _dot(b, w_ref[FOX_W:FOX_W + MLA_W, :]) + _dot(c, w_ref[FOX_W + MLA_W:, :])
    o_ref[...] = h_ref[...] + y


def _out_proj(o_fox, o_mla, o_sb, h, lw, tm):
    n = h.shape[0]

    def row(w):
        return pl.BlockSpec((tm, w), lambda i: (i, 0))

    def full(a):
        return pl.BlockSpec(a.shape, lambda i: (0, 0))

    return pl.pallas_call(
        _out_kernel,
        grid=(n // tm,),
        in_specs=[row(FOX_W), row(MLA_W), row(SB_W), row(D_MODEL), full(lw["g_of"]), full(lw["g_om"]),
                  full(lw["g_os"]), full(lw["w_out"])],
        out_specs=row(D_MODEL),
        out_shape=jax.ShapeDtypeStruct((n, D_MODEL), F32),
        compiler_params=_cparams(("arbitrary",)),
        name="out_proj",
    )(o_fox, o_mla, o_sb, h, lw["g_of"], lw["g_om"], lw["g_os"], lw["w_out"])


def _memkv_kernel(x_ref, w_ref, k_ref, v_ref):
    y = _dot(x_ref[...].astype(BF16), w_ref[0])
    k_ref[0] = y[:, :MEM_W]
    v_ref[0] = y[:, MEM_W:]


def _mem_kv(mem, w_kv, tm):
    n = mem.shape[0]
    nl = w_kv.shape[0]
    out = jax.ShapeDtypeStruct((nl, n, MEM_W), F32)
    ospec = pl.BlockSpec((1, tm, MEM_W), lambda i, l: (l, i, 0))
    return pl.pallas_call(
        _memkv_kernel,
        grid=(n // tm, nl),
        in_specs=[pl.BlockSpec((tm, D_MODEL), lambda i, l: (i, 0)),
                  pl.BlockSpec((1, D_MODEL, 2 * MEM_W), lambda i, l: (l, 0, 0))],
        out_specs=[ospec, ospec],
        out_shape=[out, out],
        compiler_params=_cparams(("arbitrary", "arbitrary")),
        name="mem_kv",
    )(mem, w_kv)


def _mem_kernel(ns, tq, h_ref, g_ref, wq_ref, k_ref, v_ref, wo_ref, o_ref):
    h = h_ref[...]
    hn = _rms(h, g_ref[...]).astype(BF16)
    q = _dot(hn, wq_ref[...]) * MEM_SCALE
    outs = []
    for hd in range(MEM_HEADS):
        sl = slice(hd * MEM_HEAD_DIM, (hd + 1) * MEM_HEAD_DIM)
        qh = q[:, sl].reshape(ns, tq, MEM_HEAD_DIM).astype(BF16)
        kh = k_ref[:, :, sl].astype(BF16)
        vh = v_ref[:, :, sl].astype(BF16)
        s = jnp.einsum("sqd,skd->sqk", qh, kh, preferred_element_type=F32)
        m = jnp.max(s, axis=-1, keepdims=True)
        p = jnp.exp(s - m)
        p = p / jnp.sum(p, axis=-1, keepdims=True)
        o = jnp.einsum("sqk,skd->sqd", p.astype(BF16), vh, preferred_element_type=F32)
        outs.append(o.reshape(ns * tq, MEM_HEAD_DIM))
    o = jnp.concatenate(outs, axis=1).astype(BF16)
    o_ref[...] = h + _dot(o, wo_ref[...])


def _mem_attn(h, mem_k, mem_v, lw, ns, tq, per_seq):
    n = h.shape[0]
    tm = ns * tq
    assert per_seq % tq == 0 and (ns == 1 or tq == per_seq)
    spb = per_seq // tq
    mt = mem_k.shape[1]

    def kv_index(i):
        return (i // spb if ns == 1 else i, 0, 0)

    def full(a):
        return pl.BlockSpec(a.shape, lambda i: (0, 0))

    return pl.pallas_call(
        functools.partial(_mem_kernel, ns, tq),
        grid=(n // tm,),
        in_specs=[pl.BlockSpec((tm, D_MODEL), lambda i: (i, 0)), full(lw["g_mem"]), full(lw["w_mq"]),
                  pl.BlockSpec((ns, mt, MEM_W), kv_index), pl.BlockSpec((ns, mt, MEM_W), kv_index),
                  full(lw["w_mo"])],
        out_specs=pl.BlockSpec((tm, D_MODEL), lambda i: (i, 0)),
        out_shape=jax.ShapeDtypeStruct((n, D_MODEL), F32),
        compiler_params=_cparams(("arbitrary",)),
        name="mem_attn",
    )(h, lw["g_mem"], lw["w_mq"], mem_k, mem_v, lw["w_mo"])


def _mlp_kernel(final, h_ref, g_ref, wu_ref, wd_ref, gfin_ref, o_ref, hn_ref, acc_ref):
    j = pl.program_id(1)

    @pl.when(j == 0)
    def _():
        hn_ref[...] = _rms(h_ref[...], g_ref[...]).astype(BF16)
        acc_ref[...] = h_ref[...]

    u = jnp.maximum(_dot(hn_ref[...], wu_ref[...]), 0.0)
    acc_ref[...] += _dot((u * u).astype(BF16), wd_ref[...])

    @pl.when(j == pl.num_programs(1) - 1)
    def _():
        y = acc_ref[...]
        o_ref[...] = _rms(y, gfin_ref[...]) if final else y


def _mlp(h, lw, g_final, final, tm, tf):
    n = h.shape[0]
    return pl.pallas_call(
        functools.partial(_mlp_kernel, final),
        grid=(n // tm, D_FF // tf),
        in_specs=[pl.BlockSpec((tm, D_MODEL), lambda i, j: (i, 0)),
                  pl.BlockSpec((1, D_MODEL), lambda i, j: (0, 0)),
                  pl.BlockSpec((D_MODEL, tf), lambda i, j: (0, j)),
                  pl.BlockSpec((tf, D_MODEL), lambda i, j: (j, 0)),
                  pl.BlockSpec((1, D_MODEL), lambda i, j: (0, 0))],
        out_specs=pl.BlockSpec((tm, D_MODEL), lambda i, j: (i, 0)),
        out_shape=jax.ShapeDtypeStruct((n, D_MODEL), F32),
        scratch_shapes=[pltpu.VMEM((tm, D_MODEL), BF16), pltpu.VMEM((tm, D_MODEL), F32)],
        compiler_params=_cparams(("arbitrary", "arbitrary")),
        name="mlp",
    )(h, lw["g_mlp"], lw["w_up"], lw["w_down"], g_final)


def _swap_halves(w, width):
    lead = w.shape[:-1]
    g = w.reshape(lead + (-1, 2, width // 2))
    return g[..., ::-1, :].reshape(w.shape)


def _lane_blocks(w, width):
    d = w.shape[0]
    g = w.reshape(d, -1, width)
    return jnp.pad(g, ((0, 0), (0, 0), (0, LANE - width))).reshape(d, -1)


def _prep_layer(l, w_in, b_f, g_mix, g_q, w_uq, g_kv, w_uk, w_uv, g_of, g_om, g_os, w_out,
                g_mem, w_mq, w_mo, g_mlp, w_up, w_down):
    sizes = (FOX_W, KV_W, KV_W, FOX_HEADS, MLA_Q_RANK, MLA_KV_RANK, MLA_ROPE, SB_W, KV_W, KV_W)
    pts, acc = [], 0
    for s in sizes[:-1]:
        acc += s
        pts.append(acc)
    fq, fk, fv, ff, mqa, mkv, mkpe, sq, sk, sv = jnp.split(w_in[l], pts, axis=1)
    d = w_in.shape[1]

    def z(w):
        return jnp.zeros((d, w), F32)

    w_in_r = jnp.concatenate(
        [fk, fv, sk, sv, mkv, mkpe, z(LANE - MLA_ROPE), _swap_halves(mkpe, MLA_ROPE), z(LANE - MLA_ROPE),
         ff, z(LANE - FOX_HEADS), mqa, z(2 * LANE - MLA_Q_RANK),
         _lane_blocks(fq, HEAD_DIM), _lane_blocks(fk, HEAD_DIM), _lane_blocks(fv, HEAD_DIM),
         _lane_blocks(sq, HEAD_DIM), _lane_blocks(sk, HEAD_DIM), _lane_blocks(sv, HEAD_DIM)], axis=1)
    assert w_in_r.shape[1] == N_INR
    uq = w_uq[l].reshape(MLA_Q_RANK, MLA_HEADS, MLA_NOPE + MLA_ROPE)
    nope = uq[:, :, :MLA_NOPE].reshape(MLA_Q_RANK, MLA_HEADS * MLA_NOPE)
    pe = uq[:, :, MLA_NOPE:].reshape(MLA_Q_RANK, MLA_HEADS * MLA_ROPE)
    w_uq_r = jnp.concatenate([nope, _lane_blocks(pe, MLA_ROPE), _lane_blocks(_swap_halves(pe, MLA_ROPE), MLA_ROPE)],
                             axis=1)
    assert w_uq_r.shape[1] == N_UQ
    w_uq_r = jnp.pad(w_uq_r, ((0, 2 * LANE - MLA_Q_RANK), (0, 0)))
    w_uk_r = jnp.pad(w_uk[l].transpose(1, 2, 0), ((0, 0), (0, 0), (0, 0)))
    return dict(
        g_mix=g_mix[l][None, :],
        w_in=w_in_r.astype(BF16),
        w_fft=jnp.pad(ff.T, ((0, 8 - FOX_HEADS), (0, 0))).astype(BF16),
        b_f=jnp.pad(b_f[l], (0, LANE - FOX_HEADS))[None, :],
        b_ft=jnp.pad(b_f[l], (0, 8 - FOX_HEADS))[:, None],
        g_q=jnp.pad(g_q[l], (0, 2 * LANE - MLA_Q_RANK))[None, :],
        w_uq=w_uq_r.astype(BF16),
        g_kv=g_kv[l][None, :],
        w_uk=w_uk_r.astype(BF16),
        w_uv=w_uv[l].transpose(1, 0, 2).astype(BF16),
        g_of=g_of[l][None, :], g_om=g_om[l][None, :], g_os=g_os[l][None, :],
        w_out=w_out[l].astype(BF16),
        g_mem=g_mem[l][None, :], w_mq=w_mq[l].astype(BF16), w_mo=w_mo[l].astype(BF16),
        g_mlp=g_mlp[l][None, :], w_up=w_up[l].astype(BF16), w_down=w_down[l].astype(BF16),
    )


def _constants():
    nblk = FOX_HEADS + 2 * FOX_KV_HEADS
    sel = np.zeros((3 * LANE, nblk * LANE), np.float32)
    ones = np.zeros((1, nblk * LANE), np.float32)
    for h in range(FOX_HEADS):
        hk, g = divmod(h, GROUP)
        for j in range(3):
            sel[j * LANE + h, h * LANE + X_C + j] = 1.0
            ones[0, h * LANE + X_G + 3 * g + j] = 1.0
            sel[j * LANE + h, (FOX_HEADS + hk) * LANE + X_G + 3 * g + j] = -1.0
    for hk in range(FOX_KV_HEADS):
        ones[0, (FOX_HEADS + hk) * LANE + X_C:(FOX_HEADS + hk) * LANE + X_C + 3] = 1.0
        ones[0, (FOX_HEADS + FOX_KV_HEADS + hk) * LANE + X_ONE_V] = 1.0
    idx = np.arange(MXU)
    later = (idx[:, None] > idx[None, :]).astype(np.float32)
    return dict(sel=jnp.asarray(sel, BF16), ones=jnp.asarray(ones, F32), later=jnp.asarray(later, BF16))


def _rope_tables(pos):
    half = MLA_ROPE // 2
    inv_freq = jnp.power(ROPE_BASE, -jnp.arange(half, dtype=F32) / half)
    ang = pos.astype(F32)[:, None] * inv_freq[None, :]
    cos, sin = jnp.cos(ang), jnp.sin(ang)
    pad = ((0, 0), (0, LANE - MLA_ROPE))
    return jnp.pad(jnp.concatenate([cos, cos], axis=1), pad), jnp.pad(jnp.concatenate([-sin, sin], axis=1), pad)


TM_IN = 512
TB_ATT = 512
TM_MLP = 1024
TF_MLP = 1024
TQ_MEM = 512
NS_MEM = 16
PP = 16


def kernel(x_prompt, x_sample, cache_fox_k, cache_fox_v, cache_fox_logf, cache_mla_ckv, cache_mla_kpe,
           cache_sb_k, cache_sb_v, cache_mem_k, cache_mem_v, page_table, mem_prompt,
           g_mix, w_in, b_fox_f, g_mla_q, w_mla_uq, g_mla_kv, w_mla_uk, w_mla_uv,
           g_out_fox, g_out_mla, g_out_sb, w_out, g_mem, w_mem_q, w_mem_k, w_mem_v, w_mem_o,
           g_mlp, w_up, w_down, g_final):
    bsz, seq, _ = x_prompt.shape
    s_cnt, n_new, _ = x_sample.shape
    depth = w_in.shape[0]
    past_len = page_table.shape[1] * PAGE
    assert n_new == N_NEW
    n_p, n_s = bsz * seq, s_cnt * n_new
    consts = _constants()

    cos_p, sin_p = _rope_tables(jnp.arange(seq, dtype=jnp.int32))
    cos_p, sin_p = jnp.tile(cos_p, (bsz, 1)), jnp.tile(sin_p, (bsz, 1))
    cos_s, sin_s = _rope_tables(past_len + jnp.arange(n_new, dtype=jnp.int32))
    cos_s, sin_s = jnp.tile(cos_s, (s_cnt, 1)), jnp.tile(sin_s, (s_cnt, 1))

    pool = cache_fox_k.shape[1]

    def keys_minor(c):
        return c.transpose(0, 1, 3, 4, 2).reshape(depth, pool, KV_W, PAGE)

    caches = (keys_minor(cache_fox_k), keys_minor(cache_fox_v), cache_fox_logf.transpose(0, 3, 1, 2),
              cache_mla_ckv, cache_mla_kpe.transpose(0, 1, 3, 2), keys_minor(cache_sb_k), keys_minor(cache_sb_v))
    mem_tok = mem_prompt.shape[1]
    w_kv = jnp.concatenate([w_mem_k, w_mem_v], axis=2).astype(BF16)
    p_mem_k, p_mem_v = _mem_kv(mem_prompt.reshape(bsz * mem_tok, D_MODEL), w_kv, 512)
    s_mem_k = cache_mem_k.reshape(depth, s_cnt, mem_tok, MEM_W)
    s_mem_v = cache_mem_v.reshape(depth, s_cnt, mem_tok, MEM_W)
    g_fin = g_final[None, :]

    hp = x_prompt.reshape(n_p, D_MODEL)
    hs = x_sample.reshape(n_s, D_MODEL)
    rows_p, rows_s = [], []
    tm_p, tm_s = min(TM_IN, seq), min(TM_IN, n_s)
    tmlp_p, tmlp_s = min(TM_MLP, n_p), min(TM_MLP, n_s)
    tb = min(TB_ATT, seq)
    tq_mem = min(TQ_MEM, seq)
    ns_mem = min(NS_MEM, s_cnt)
    pp = min(PP, page_table.shape[1])
    for l in range(depth):
        lw = _prep_layer(l, w_in, b_fox_f, g_mix, g_mla_q, w_mla_uq, g_mla_kv, w_mla_uk, w_mla_uv,
                         g_out_fox, g_out_mla, g_out_sb, w_out, g_mem, w_mem_q, w_mem_o, g_mlp, w_up, w_down)
        final = l == depth - 1

        r = _in_proj(hp, lw, consts, cos_p, sin_p, seq, tm_p)
        o_fox, o_mla, o_sb = _prompt_attn(bsz, seq, tb, r, consts["later"], lw["w_uv"])
        hp = _out_proj(o_fox, o_mla, o_sb, hp, lw, tm_p)
        hp = _mem_attn(hp, p_mem_k[l].reshape(bsz, mem_tok, MEM_W), p_mem_v[l].reshape(bsz, mem_tok, MEM_W),
                       lw, 1, tq_mem, seq)
        hp = _mlp(hp, lw, g_fin, final, tmlp_p, TF_MLP)
        rows_p.append(r)

        r = _in_proj(hs, lw, consts, cos_s, sin_s, n_new, tm_s)
        qops, new_ops = _sample_operands(r, s_cnt)
        ofs, om = _sample_attn(l, page_table, qops, new_ops, consts["later"], caches, pp)
        o_fox, o_lat, o_sb = _sample_unpack(ofs, om, s_cnt)
        o_mla = _uv_proj(o_lat, lw["w_uv"])
        hs = _out_proj(o_fox, o_mla, o_sb, hs, lw, tm_s)
        hs = _mem_attn(hs, s_mem_k[l], s_mem_v[l], lw, ns_mem, n_new, n_new)
        hs = _mlp(hs, lw, g_fin, final, tmlp_s, TF_MLP)
        rows_s.append(r)

    def stack(rows, key, shape):
        return jnp.stack([r[key] for r in rows]).reshape((depth,) + shape)

    def group_rows(rows, lead):
        return (stack(rows, "fk32", lead + (FOX_KV_HEADS, HEAD_DIM)), stack(rows, "fv32", lead + (FOX_KV_HEADS, HEAD_DIM)),
                stack(rows, "logf", lead + (FOX_HEADS,)), stack(rows, "ckv32", lead + (MLA_KV_RANK,)),
                stack(rows, "kpe32", lead + (MLA_ROPE,)),
                stack(rows, "sk32", lead + (SB_KV_HEADS, HEAD_DIM)), stack(rows, "sv32", lead + (SB_KV_HEADS, HEAD_DIM)))

    y_prompt = hp.reshape(bsz, seq, D_MODEL)
    y_sample = hs.reshape(s_cnt, n_new, D_MODEL)
    pm_shape = (depth, bsz, mem_tok, MEM_HEADS, MEM_HEAD_DIM)
    return ((y_prompt, y_sample) + group_rows(rows_p, (bsz, seq))
            + (p_mem_k.reshape(pm_shape), p_mem_v.reshape(pm_shape)) + group_rows(rows_s, (s_cnt, n_new)))
```

```python
import functools
import math

import jax
import jax.numpy as jnp
import numpy as np
from jax import lax
from jax.experimental import pallas as pl
from jax.experimental.pallas import tpu as pltpu

F32 = jnp.float32
BF16 = jnp.bfloat16

D_MODEL = 1024
HEAD_DIM = 64
FOX_HEADS = 6
FOX_KV_HEADS = 2
SB_HEADS = 6
SB_KV_HEADS = 2
GROUP = 3
MLA_HEADS = 4
MLA_Q_RANK = 192
MLA_KV_RANK = 128
MLA_NOPE = 64
MLA_ROPE = 32
MLA_V = 64
ROPE_BASE = 10000.0
MEM_HEADS = 4
MEM_HEAD_DIM = 64
MEM_W = MEM_HEADS * MEM_HEAD_DIM
D_FF = 4 * D_MODEL
EPS = 1e-6
PAGE = 128
N_NEW = 8
FOX_W = FOX_HEADS * HEAD_DIM
MLA_W = MLA_HEADS * MLA_V
SB_W = SB_HEADS * HEAD_DIM
KV_W = FOX_KV_HEADS * HEAD_DIM
ATT_SCALE = 1.0 / math.sqrt(HEAD_DIM)
MLA_SCALE = 1.0 / math.sqrt(MLA_NOPE + MLA_ROPE)
MEM_SCALE = 1.0 / math.sqrt(MEM_HEAD_DIM)
LOG2E = math.log2(math.e)
MLA_EXP2 = MLA_SCALE * LOG2E

LANE = 128
MXU = 256
VMEM_LIMIT = 48 * 1024 * 1024

X_C = HEAD_DIM
X_G = HEAD_DIM + 3
X_ONE_V = HEAD_DIM
X_ONE_M = MLA_KV_RANK + MLA_ROPE

C_FK = 0
C_FV = C_FK + KV_W
C_SK = C_FV + KV_W
C_SV = C_SK + KV_W
C_CKV = C_SV + KV_W
C_KPE = C_CKV + MLA_KV_RANK
C_KPS = C_KPE + LANE
C_FF = C_KPS + LANE
C_MQA = C_FF + LANE
C_PQF = C_MQA + 2 * LANE
C_PKF = C_PQF + FOX_HEADS * LANE
C_PVF = C_PKF + FOX_KV_HEADS * LANE
C_PQS = C_PVF + FOX_KV_HEADS * LANE
C_PKS = C_PQS + SB_HEADS * LANE
C_PVS = C_PKS + SB_KV_HEADS * LANE
N_INR = C_PVS + SB_KV_HEADS * LANE

U_NOPE = 0
U_PE = MLA_HEADS * MLA_NOPE
U_PS = U_PE + MLA_HEADS * LANE
N_UQ = U_PS + MLA_HEADS * LANE

NEG_INF = float("-inf")


def _cparams(sem):
    return pltpu.CompilerParams(dimension_semantics=sem, vmem_limit_bytes=VMEM_LIMIT)


def _log_sigmoid(x):
    return jnp.minimum(x, 0.0) - jnp.log(1.0 + jnp.exp(-jnp.abs(x)))


def _split3(x):
    hi = x.astype(BF16)
    r = x - hi.astype(F32)
    mid = r.astype(BF16)
    lo = (r - mid.astype(F32)).astype(BF16)
    return hi, mid, lo


def _split2(x):
    hi = x.astype(BF16)
    lo = (x - hi.astype(F32)).astype(BF16)
    return hi, lo


def _dot(a, b):
    return jnp.dot(a, b, preferred_element_type=F32)


def _dot_nt(a, b):
    return lax.dot_general(a, b, (((1,), (1,)), ((), ())), preferred_element_type=F32)


def _rms(x, g, n=None):
    n = x.shape[-1] if n is None else n
    ms = jnp.sum(x * x, axis=-1, keepdims=True) * (1.0 / n)
    return x * lax.rsqrt(ms + EPS) * g


def _in_kernel(seg, tm, h_ref, g_ref, w_ref, wfft_ref, bf_ref, bft_ref, gq_ref, wuq_ref, gkv_ref,
               wuk_ref, cos_ref, sin_ref, sel_ref, one_ref,
               fk32, fv32, sk32, sv32, ckv32, kpe32, logf_o, logft_o,
               qf_o, kf_o, vf_o, qs_o, ks_o, vs_o, qm_o, kvm_o, carry_ref):
    i = pl.program_id(0)
    xn = _rms(h_ref[...], g_ref[...]).astype(BF16)
    proj = _dot(xn, w_ref[...])

    fk32[...] = proj[:, C_FK:C_FK + KV_W]
    fv32[...] = proj[:, C_FV:C_FV + KV_W]
    sk32[...] = proj[:, C_SK:C_SK + KV_W]
    sv32[...] = proj[:, C_SV:C_SV + KV_W]

    logf = _log_sigmoid(proj[:, C_FF:C_FF + LANE] + bf_ref[...])
    logf_o[...] = logf[:, :FOX_HEADS]
    logft_o[...] = _log_sigmoid(_dot_nt(wfft_ref[...], xn) + bft_ref[...])
    r = lax.broadcasted_iota(jnp.int32, (tm, tm), 0)
    cidx = lax.broadcasted_iota(jnp.int32, (tm, tm), 1)
    lower = cidx <= r
    if seg < tm:
        lower = lower & ((r // seg) == (cidx // seg))
    lower = jnp.where(lower, 1.0, 0.0).astype(BF16)
    hi, mid, lo = _split3(logf)
    c3 = _dot(lower, jnp.concatenate([hi, mid, lo], axis=1))
    c = (c3[:, :LANE] + c3[:, LANE:2 * LANE]) + c3[:, 2 * LANE:]
    if seg > tm:
        @pl.when(i % (seg // tm) == 0)
        def _():
            carry_ref[...] = jnp.zeros_like(carry_ref)

        c = c + carry_ref[...]
        carry_ref[...] = c[tm - 1:tm, :]
    hi, mid, lo = _split3(c)
    ext = _dot(jnp.concatenate([hi, mid, lo], axis=1), sel_ref[...]) + one_ref[...]
    nq = FOX_HEADS * LANE
    nk = FOX_KV_HEADS * LANE
    qf_o[...] = (proj[:, C_PQF:C_PQF + nq] * ATT_SCALE + ext[:, :nq]).astype(BF16)
    kf_o[...] = (proj[:, C_PKF:C_PKF + nk] + ext[:, nq:nq + nk]).astype(BF16)
    vf_o[...] = (proj[:, C_PVF:C_PVF + nk] + ext[:, nq + nk:]).astype(BF16)
    qs_o[...] = (proj[:, C_PQS:C_PQS + nq] * ATT_SCALE).astype(BF16)
    ks_o[...] = proj[:, C_PKS:C_PKS + nk].astype(BF16)
    vs_o[...] = proj[:, C_PVS:C_PVS + nk].astype(BF16)

    ckv = _rms(proj[:, C_CKV:C_CKV + MLA_KV_RANK], gkv_ref[...])
    ckv32[...] = ckv
    cos = cos_ref[...]
    sin = sin_ref[...]
    kpe = proj[:, C_KPE:C_KPE + LANE] * cos + proj[:, C_KPS:C_KPS + LANE] * sin
    kpe32[...] = kpe[:, :MLA_ROPE]
    lane = lax.broadcasted_iota(jnp.int32, (1, LANE), 1)
    kvm_o[:, :LANE] = ckv.astype(BF16)
    kvm_o[:, LANE:] = (kpe + jnp.where(lane == MLA_ROPE, 1.0, 0.0)).astype(BF16)
    qa = _rms(proj[:, C_MQA:C_MQA + 2 * LANE], gq_ref[...], MLA_Q_RANK).astype(BF16)
    q = _dot(qa, wuq_ref[...])
    for hh in range(MLA_HEADS):
        qn = q[:, U_NOPE + hh * MLA_NOPE:U_NOPE + (hh + 1) * MLA_NOPE].astype(BF16)
        qm_o[:, hh * MXU:hh * MXU + LANE] = _dot(qn, wuk_ref[hh]).astype(BF16)
        qpe = (q[:, U_PE + hh * LANE:U_PE + (hh + 1) * LANE] * cos
               + q[:, U_PS + hh * LANE:U_PS + (hh + 1) * LANE] * sin)
        qm_o[:, hh * MXU + LANE:(hh + 1) * MXU] = qpe.astype(BF16)


def _in_proj(h, lw, consts, cos, sin, seg, tm):
    n = h.shape[0]
    assert n % tm == 0 and (seg % tm == 0 or tm % seg == 0)

    def row(w):
        return pl.BlockSpec((tm, w), lambda i: (i, 0))

    def full(a):
        nd = a.ndim
        return pl.BlockSpec(a.shape, lambda i, _nd=nd: (0,) * _nd)

    ins = [h, lw["g_mix"], lw["w_in"], lw["w_fft"], lw["b_f"], lw["b_ft"], lw["g_q"], lw["w_uq"],
           lw["g_kv"], lw["w_uk"], cos, sin, consts["sel"], consts["ones"]]
    in_specs = [row(D_MODEL)] + [full(a) for a in ins[1:10]] + [row(LANE), row(LANE), full(ins[12]), full(ins[13])]
    outs = [
        ("fk32", KV_W, F32), ("fv32", KV_W, F32), ("sk32", KV_W, F32), ("sv32", KV_W, F32),
        ("ckv32", MLA_KV_RANK, F32), ("kpe32", MLA_ROPE, F32), ("logf", FOX_HEADS, F32), ("logft", None, F32),
        ("qf", FOX_HEADS * LANE, BF16), ("kf", FOX_KV_HEADS * LANE, BF16), ("vf", FOX_KV_HEADS * LANE, BF16),
        ("qs", SB_HEADS * LANE, BF16), ("ks", SB_KV_HEADS * LANE, BF16), ("vs", SB_KV_HEADS * LANE, BF16),
        ("qm", MLA_HEADS * MXU, BF16), ("kvm", MXU, BF16),
    ]
    out_shape, out_specs = [], []
    for _, w, dt in outs:
        if w is None:
            out_shape.append(jax.ShapeDtypeStruct((8, n), dt))
            out_specs.append(pl.BlockSpec((8, tm), lambda i: (0, i)))
        else:
            out_shape.append(jax.ShapeDtypeStruct((n, w), dt))
            out_specs.append(row(w))
    res = pl.pallas_call(
        functools.partial(_in_kernel, seg, tm),
        grid=(n // tm,),
        in_specs=in_specs,
        out_specs=out_specs,
        out_shape=out_shape,
        scratch_shapes=[pltpu.VMEM((1, LANE), F32)],
        compiler_params=_cparams(("arbitrary",)),
        name="in_proj",
    )(*ins)
    return dict(zip([o[0] for o in outs], res))


def _pair_tables(nq):
    qi, kj = [], []
    for i in range(nq):
        for j in range(i, -1, -1):
            qi.append(i)
            kj.append(j)
    return jnp.asarray(qi, jnp.int32), jnp.asarray(kj, jnp.int32)


def _heads_on_rows(ref, first, count, width):
    return jnp.concatenate([ref[:, (first + g) * width:(first + g + 1) * width] for g in range(count)], axis=0)


def _attn_kernel(tb, qi_ref, kj_ref, qf_ref, kf_ref, vf_ref, qs_ref, ks_ref, vs_ref, qm_ref, kvm_ref,
                 later_ref, wuv_ref, of_ref, om_ref, os_ref,
                 fm_ref, facc_ref, mm_ref, macc_ref, sacc_ref, scar_ref):
    p = pl.program_id(1)
    qi = qi_ref[p]
    kj = kj_ref[p]

    @pl.when(kj == qi)
    def _():
        fm_ref[...] = jnp.full_like(fm_ref, NEG_INF)
        mm_ref[...] = jnp.full_like(mm_ref, NEG_INF)
        facc_ref[...] = jnp.zeros_like(facc_ref)
        macc_ref[...] = jnp.zeros_like(macc_ref)
        sacc_ref[...] = jnp.zeros_like(sacc_ref)
        scar_ref[...] = jnp.zeros_like(scar_ref)

    def body(diag):
        def masks(rows):
            r = lax.broadcasted_iota(jnp.int32, (rows, tb), 0) % tb
            c = lax.broadcasted_iota(jnp.int32, (rows, tb), 1)
            return c <= r, c < r

        if diag:
            allow_g, strict_g = masks(GROUP * tb)
            allow_m, _ = masks(MLA_HEADS * tb)

        for hk in range(FOX_KV_HEADS):
            q = _heads_on_rows(qf_ref, hk * GROUP, GROUP, LANE)
            s = _dot_nt(q, kf_ref[:, hk * LANE:(hk + 1) * LANE])
            if diag:
                s = jnp.where(allow_g, s, NEG_INF)
            m_prev = fm_ref[hk]
            m_new = jnp.maximum(m_prev, jnp.max(s, axis=1, keepdims=True))
            pm = jnp.exp(s - m_new).astype(BF16)
            facc_ref[hk] = jnp.exp(m_prev - m_new) * facc_ref[hk] + _dot(pm, vf_ref[:, hk * LANE:(hk + 1) * LANE])
            fm_ref[hk] = m_new

        q = _heads_on_rows(qm_ref, 0, MLA_HEADS, MXU)
        kv = kvm_ref[...]
        s = _dot_nt(q, kv)
        if diag:
            s = jnp.where(allow_m, s, NEG_INF)
        m_prev = mm_ref[...]
        m_new = jnp.maximum(m_prev, jnp.max(s, axis=1, keepdims=True))
        pm = jnp.exp2((s - m_new) * MLA_EXP2).astype(BF16)
        macc_ref[...] = jnp.exp2((m_prev - m_new) * MLA_EXP2) * macc_ref[...] + _dot(pm, kv)
        mm_ref[...] = m_new

        later = later_ref[...]
        rows = GROUP * tb
        for hk in range(SB_KV_HEADS):
            q = _heads_on_rows(qs_ref, hk * GROUP, GROUP, LANE)
            z = _dot_nt(q, ks_ref[:, hk * LANE:(hk + 1) * LANE]) * LOG2E
            ls = jnp.minimum(z, 0.0) - jnp.log2(1.0 + jnp.exp2(-jnp.abs(z)))
            lk = ls - z
            if diag:
                lk = jnp.where(strict_g, lk, 0.0)
            hi, lo = _split2(lk)
            hl = jnp.concatenate([hi, lo], axis=0)
            car = scar_ref[hk]
            parts = [None] * (tb // MXU)
            for t in range(tb // MXU - 1, -1, -1):
                sf = _dot(hl[:, t * MXU:(t + 1) * MXU], later)
                sf = sf[:rows] + sf[rows:]
                parts[t] = sf + car
                car = car + (sf[:, 0:1] + lk[:, t * MXU:t * MXU + 1])
            scar_ref[hk] = car
            sfx = parts[0] if len(parts) == 1 else jnp.concatenate(parts, axis=1)
            a = jnp.exp2(ls + sfx)
            if diag:
                a = jnp.where(strict_g, a, 0.0)
            sacc_ref[hk] = sacc_ref[hk] + _dot(a.astype(BF16), vs_ref[:, hk * LANE:(hk + 1) * LANE])

    @pl.when(kj == qi)
    def _():
        body(True)

    @pl.when(kj < qi)
    def _():
        body(False)

    @pl.when(kj == 0)
    def _():
        for hk in range(FOX_KV_HEADS):
            for g in range(GROUP):
                h = hk * GROUP + g
                rows = slice(g * tb, (g + 1) * tb)
                of_ref[:, h * HEAD_DIM:(h + 1) * HEAD_DIM] = (
                    facc_ref[hk, rows, 0:HEAD_DIM] / facc_ref[hk, rows, X_ONE_V:X_ONE_V + 1])
                os_ref[:, h * HEAD_DIM:(h + 1) * HEAD_DIM] = sacc_ref[hk, rows, 0:HEAD_DIM]
        for h in range(MLA_HEADS):
            rows = slice(h * tb, (h + 1) * tb)
            o_lat = (macc_ref[rows, 0:MLA_KV_RANK] / macc_ref[rows, X_ONE_M:X_ONE_M + 1]).astype(BF16)
            om_ref[:, h * MLA_V:(h + 1) * MLA_V] = _dot(o_lat, wuv_ref[h])


def _prompt_attn(bsz, seq, tb, r, later, wuv):
    nq = seq // tb
    qi_t, kj_t = _pair_tables(nq)
    n = bsz * seq

    def qspec(w):
        return pl.BlockSpec((tb, w), lambda b, p, qi, kj: (b * nq + qi[p], 0))

    def kspec(w):
        return pl.BlockSpec((tb, w), lambda b, p, qi, kj: (b * nq + kj[p], 0))

    nq_l, nk_l = FOX_HEADS * LANE, FOX_KV_HEADS * LANE
    in_specs = [qspec(nq_l), kspec(nk_l), kspec(nk_l), qspec(nq_l), kspec(nk_l), kspec(nk_l),
                qspec(MLA_HEADS * MXU), kspec(MXU),
                pl.BlockSpec(later.shape, lambda b, p, qi, kj: (0, 0)),
                pl.BlockSpec(wuv.shape, lambda b, p, qi, kj: (0, 0, 0))]
    scratch = [pltpu.VMEM((FOX_KV_HEADS, GROUP * tb, 1), F32), pltpu.VMEM((FOX_KV_HEADS, GROUP * tb, LANE), F32),
               pltpu.VMEM((MLA_HEADS * tb, 1), F32), pltpu.VMEM((MLA_HEADS * tb, MXU), F32),
               pltpu.VMEM((SB_KV_HEADS, GROUP * tb, LANE), F32), pltpu.VMEM((SB_KV_HEADS, GROUP * tb, 1), F32)]
    return pl.pallas_call(
        functools.partial(_attn_kernel, tb),
        grid_spec=pltpu.PrefetchScalarGridSpec(
            num_scalar_prefetch=2,
            grid=(bsz, int(qi_t.shape[0])),
            in_specs=in_specs,
            out_specs=[qspec(FOX_W), qspec(MLA_W), qspec(SB_W)],
            scratch_shapes=scratch),
        out_shape=[jax.ShapeDtypeStruct((n, FOX_W), F32), jax.ShapeDtypeStruct((n, MLA_W), F32),
                   jax.ShapeDtypeStruct((n, SB_W), F32)],
        compiler_params=_cparams(("arbitrary", "arbitrary")),
        name="prompt_attn",
    )(qi_t, kj_t, r["qf"], r["kf"], r["vf"], r["qs"], r["ks"], r["vs"], r["qm"], r["kvm"], later, wuv)


R_FOX = FOX_HEADS * N_NEW
R_FS = R_FOX + SB_HEADS * N_NEW
R_MLA = MLA_HEADS * N_NEW
R_CUM = SB_HEADS * N_NEW + 8
N_ARR = 7
SUB_PAGES = 16


def _sample_kernel(pp, nch, layer, pt_ref, qfs_ref, qm1_ref, qm2_ref, nk_ref, nv_ref, nckv_ref, nkpe_ref, nlf_ref,
                   later_ref, *rest):
    caches = rest[:N_ARR]
    ofs_ref, om_ref = rest[N_ARR:N_ARR + 2]
    mf_ref, lf_ref, mm_ref, lm_ref, accfs_ref, accm_ref, carry_ref = rest[N_ARR + 2:N_ARR + 9]
    bufs = rest[N_ARR + 9:2 * N_ARR + 9]
    sem = rest[2 * N_ARR + 9]
    b = pl.program_id(0)
    cstep = pl.program_id(1)
    step = b * nch + cstep
    slot = step % 2
    qfs = qfs_ref[0]
    qm1 = qm1_ref[0]
    qm2 = qm2_ref[0]

    nsub = pp // SUB_PAGES

    def sub_copies(bb, cc, sl, s):
        out = []
        for i in range(s * SUB_PAGES, (s + 1) * SUB_PAGES):
            pg = pt_ref[bb, (nch - 1 - cc) * pp + i]
            for a in range(N_ARR):
                if a == 2:
                    src = caches[a].at[layer, :, pl.ds(pl.multiple_of((pg // 8) * 8, 8), 8), :]
                else:
                    src = caches[a].at[layer, pg]
                out.append(pltpu.make_async_copy(src, bufs[a].at[sl, i], sem.at[sl, s, a]))
        return out

    def start_all(copies):
        for k, cp in enumerate(copies):
            cp.start(priority=k % 2)

    @pl.when(step == 0)
    def _():
        for s in range(nsub - 1, -1, -1):
            start_all(sub_copies(b, cstep, slot, s))

    last_c = cstep == nch - 1
    last_step = step == pl.num_programs(0) * nch - 1
    nb = jnp.where(last_c & jnp.logical_not(last_step), b + 1, b)
    nc = jnp.where(last_step, cstep, jnp.where(last_c, 0, cstep + 1))
    for s in range(nsub - 1, -1, -1):
        start_all(sub_copies(nb, nc, 1 - slot, s))

    def block(s_fs, s_m, lf8, width, new, pv_fs, pv_m):
        z = s_fs[R_FOX:]
        ls = jnp.minimum(z, 0.0) - jnp.log(1.0 + jnp.exp(-jnp.abs(z)))
        lk = ls - z
        if new:
            key = lax.broadcasted_iota(jnp.int32, (R_FOX, width), 1)
            tok = lax.broadcasted_iota(jnp.int32, (R_FOX, width), 0) % N_NEW
            strict = key < tok
            allow = key <= tok
            keym = lax.broadcasted_iota(jnp.int32, (R_MLA, width), 1)
            allow_m = keym <= (lax.broadcasted_iota(jnp.int32, (R_MLA, width), 0) % N_NEW)
            lk = jnp.where(strict, lk, 0.0)
        y = jnp.concatenate([lk, lf8], axis=0)
        hi, lo = _split2(y)
        hl = jnp.concatenate([hi, lo], axis=0)
        tile = min(width, MXU)
        nt = width // tile
        later = later_ref[0:tile, 0:tile]
        sfs, tots = [], []
        for t in range(nt):
            sf = _dot(hl[:, t * tile:(t + 1) * tile], later)
            sf = sf[:R_CUM] + sf[R_CUM:]
            sfs.append(sf)
            tots.append(sf[:, 0:1] + y[:, t * tile:t * tile + 1])
        car = carry_ref[...]
        parts = [None] * nt
        for t in range(nt - 1, -1, -1):
            parts[t] = sfs[t] + car
            car = car + tots[t]
        carry_ref[...] = car
        sfx = parts[0] if nt == 1 else jnp.concatenate(parts, axis=1)
        decay = jnp.concatenate(
            [jnp.broadcast_to(sfx[R_CUM - 8 + h:R_CUM - 7 + h], (N_NEW, width)) for h in range(FOX_HEADS)], axis=0)
        tf = s_fs[:R_FOX] + decay
        a = jnp.exp(ls + sfx[:R_CUM - 8])
        if new:
            tf = jnp.where(allow, tf, NEG_INF)
            s_m = jnp.where(allow_m, s_m, NEG_INF)
            a = jnp.where(strict, a, 0.0)
        m_prev = mf_ref[...]
        m_new = jnp.maximum(m_prev, jnp.max(tf, axis=1, keepdims=True))
        alpha = jnp.exp(m_prev - m_new)
        pf = jnp.exp(tf - m_new)
        lf_ref[...] = alpha * lf_ref[...] + jnp.sum(pf, axis=1, keepdims=True)
        mf_ref[...] = m_new
        pfs = jnp.concatenate([pf, a], axis=0).astype(BF16)
        o = pv_fs(pfs)
        accfs_ref[0:R_FOX, :] = alpha * accfs_ref[0:R_FOX, :] + o[:R_FOX]
        accfs_ref[R_FOX:, :] = accfs_ref[R_FOX:, :] + o[R_FOX:]
        m_prev = mm_ref[...]
        m_new = jnp.maximum(m_prev, jnp.max(s_m, axis=1, keepdims=True))
        alpha = jnp.exp2((m_prev - m_new) * MLA_EXP2)
        pm = jnp.exp2((s_m - m_new) * MLA_EXP2)
        lm_ref[...] = alpha * lm_ref[...] + jnp.sum(pm, axis=1, keepdims=True)
        mm_ref[...] = m_new
        accm_ref[...] = alpha * accm_ref[...] + pv_m(pm.astype(BF16))

    @pl.when(cstep == 0)
    def _():
        mf_ref[...] = jnp.full_like(mf_ref, NEG_INF)
        mm_ref[...] = jnp.full_like(mm_ref, NEG_INF)
        lf_ref[...] = jnp.zeros_like(lf_ref)
        lm_ref[...] = jnp.zeros_like(lm_ref)
        accfs_ref[...] = jnp.zeros_like(accfs_ref)
        accm_ref[...] = jnp.zeros_like(accm_ref)
        carry_ref[...] = jnp.zeros_like(carry_ref)
        nk = nk_ref[0].astype(BF16)
        nv = nv_ref[0].astype(BF16)
        nckv = nckv_ref[0].astype(BF16)
        nkpe = nkpe_ref[0].astype(BF16)
        block(_dot_nt(qfs, nk), _dot_nt(qm1, nckv) + _dot_nt(qm2, nkpe), nlf_ref[0], PAGE, True,
              lambda pfs: _dot(pfs, nv), lambda pm: _dot(pm, nckv))

    for s in range(nsub - 1, -1, -1):
        for cp in sub_copies(b, cstep, slot, s):
            cp.wait()
        pages = range(s * SUB_PAGES, (s + 1) * SUB_PAGES)

        def cat(a, axis, pages=pages):
            return jnp.concatenate([bufs[a][slot, i].astype(BF16) for i in pages], axis=axis)

        kfs = jnp.concatenate([cat(0, 1), cat(5, 1)], axis=0)
        vfs = jnp.concatenate([cat(1, 1), cat(6, 1)], axis=0)
        ckv = cat(3, 0)
        kpet = cat(4, 1)
        lfs = []
        for i in pages:
            row = pt_ref[b, (nch - 1 - cstep) * pp + i] % 8
            six = jnp.concatenate([bufs[2][slot, i, h, pl.ds(row, 1), :] for h in range(FOX_HEADS)], axis=0)
            lfs.append(jnp.concatenate([six, jnp.zeros((8 - FOX_HEADS, PAGE), F32)], axis=0))
        lf8 = jnp.concatenate(lfs, axis=1)
        block(_dot(qfs, kfs), _dot_nt(qm1, ckv) + _dot(qm2, kpet), lf8, SUB_PAGES * PAGE, False,
              lambda pfs, vfs=vfs: _dot_nt(pfs, vfs), lambda pm, ckv=ckv: _dot(pm, ckv))

    @pl.when(cstep == nch - 1)
    def _():
        inv = 1.0 / lf_ref[...]
        ofs_ref[0, 0:R_FOX, :] = accfs_ref[0:R_FOX, :] * inv
        ofs_ref[0, R_FOX:, :] = accfs_ref[R_FOX:, :]
        om_ref[0] = accm_ref[...] * (1.0 / lm_ref[...])

    @pl.when(last_step)
    def _():
        for s in range(nsub):
            for cp in sub_copies(nb, nc, 1 - slot, s):
                cp.wait()


def _sample_attn(layer, page_table, qops, new_ops, later, caches, pp):
    s_cnt, n_pages = page_table.shape
    assert n_pages % pp == 0 and pp % SUB_PAGES == 0
    nch = n_pages // pp

    def per_sample(a):
        return pl.BlockSpec((1,) + a.shape[1:], lambda b, c, pt: (b, 0, 0))

    in_specs = [per_sample(a) for a in qops] + [per_sample(a) for a in new_ops]
    in_specs.append(pl.BlockSpec(later.shape, lambda b, c, pt: (0, 0)))
    in_specs += [pl.BlockSpec(memory_space=pl.ANY)] * N_ARR
    bufs = [pltpu.VMEM((2, pp, FOX_HEADS, 8, PAGE) if k == 2 else (2, pp) + a.shape[2:], F32)
            for k, a in enumerate(caches)]
    return pl.pallas_call(
        functools.partial(_sample_kernel, pp, nch, layer),
        grid_spec=pltpu.PrefetchScalarGridSpec(
            num_scalar_prefetch=1,
            grid=(s_cnt, nch),
            in_specs=in_specs,
            out_specs=[pl.BlockSpec((1, R_FS, MXU), lambda b, c, pt: (b, 0, 0)),
                       pl.BlockSpec((1, R_MLA, LANE), lambda b, c, pt: (b, 0, 0))],
            scratch_shapes=[pltpu.VMEM((R_FOX, 1), F32), pltpu.VMEM((R_FOX, 1), F32),
                            pltpu.VMEM((R_MLA, 1), F32), pltpu.VMEM((R_MLA, 1), F32),
                            pltpu.VMEM((R_FS, MXU), F32), pltpu.VMEM((R_MLA, LANE), F32),
                            pltpu.VMEM((R_CUM, 1), F32)] + bufs
            + [pltpu.SemaphoreType.DMA((2, pp // SUB_PAGES, N_ARR))]),
        out_shape=[jax.ShapeDtypeStruct((s_cnt, R_FS, MXU), F32), jax.ShapeDtypeStruct((s_cnt, R_MLA, LANE), F32)],
        compiler_params=_cparams(("arbitrary", "arbitrary")),
        name="sample_attn",
    )(page_table, *qops, *new_ops, later, *caches)


def _sample_operands(r, s_cnt):
    def slots(q, nh, w, keep):
        return q.reshape(s_cnt, N_NEW, nh, w)[..., :keep].transpose(0, 2, 1, 3).reshape(s_cnt, nh * N_NEW, keep)

    def gqa(q):
        z = jnp.zeros_like(q)
        half = GROUP * N_NEW
        return jnp.concatenate([jnp.concatenate([q[:, :half], z[:, :half]], axis=2),
                                jnp.concatenate([z[:, half:], q[:, half:]], axis=2)], axis=1)

    qf = gqa(slots(r["qf"], FOX_HEADS, LANE, HEAD_DIM))
    qs = gqa(slots(r["qs"], SB_HEADS, LANE, HEAD_DIM))
    zf = jnp.zeros_like(qf)
    qfs = jnp.concatenate([jnp.concatenate([qf, zf], axis=2), jnp.concatenate([zf, qs], axis=2)], axis=1)
    qm = r["qm"].reshape(s_cnt, N_NEW, MLA_HEADS, MXU).transpose(0, 2, 1, 3).reshape(s_cnt, R_MLA, MXU)
    qm1, qm2 = qm[..., :MLA_KV_RANK], qm[..., MLA_KV_RANK:MLA_KV_RANK + MLA_ROPE]

    def pad_keys(a):
        a = a.reshape(s_cnt, N_NEW, a.shape[-1])
        return jnp.pad(a, ((0, 0), (0, PAGE - N_NEW), (0, 0)))

    nk = pad_keys(jnp.concatenate([r["fk32"], r["sk32"]], axis=1))
    nv = pad_keys(jnp.concatenate([r["fv32"], r["sv32"]], axis=1))
    nlf = r["logft"].reshape(8, s_cnt, N_NEW).transpose(1, 0, 2)
    nlf = jnp.where(jnp.arange(8)[None, :, None] < FOX_HEADS, nlf, 0.0)
    nlf = jnp.pad(nlf, ((0, 0), (0, 0), (0, PAGE - N_NEW)))
    return (qfs, qm1, qm2), (nk, nv, pad_keys(r["ckv32"]), pad_keys(r["kpe32"]), nlf)


def _sample_unpack(ofs, om, s_cnt):
    def gqa(a):
        a = a.reshape(s_cnt, FOX_KV_HEADS, GROUP, N_NEW, FOX_KV_HEADS, HEAD_DIM)
        a = jnp.stack([a[:, hk, :, :, hk, :] for hk in range(FOX_KV_HEADS)], axis=1)
        return a.transpose(0, 3, 1, 2, 4).reshape(s_cnt * N_NEW, FOX_W)

    o_fox = gqa(ofs[:, :R_FOX, :LANE])
    o_sb = gqa(ofs[:, R_FOX:, LANE:])
    o_lat = om.reshape(s_cnt, MLA_HEADS, N_NEW, MLA_KV_RANK).transpose(0, 2, 1, 3)
    return o_fox, o_lat.reshape(s_cnt * N_NEW, MLA_HEADS * MLA_KV_RANK), o_sb


def _uv_kernel(x_ref, w_ref, o_ref):
    for h in range(MLA_HEADS):
        x = x_ref[:, h * MLA_KV_RANK:(h + 1) * MLA_KV_RANK].astype(BF16)
        o_ref[:, h * MLA_V:(h + 1) * MLA_V] = _dot(x, w_ref[h])


def _uv_proj(o_lat, wuv):
    n = o_lat.shape[0]
    return pl.pallas_call(
        _uv_kernel,
        out_shape=jax.ShapeDtypeStruct((n, MLA_W), F32),
        name="uv_proj",
    )(o_lat, wuv)


def _out_kernel(of_ref, om_ref, os_ref, h_ref, gf_ref, gm_ref, gs_ref, w_ref, o_ref):
    a = _rms(of_ref[...], gf_ref[...]).astype(BF16)
    b = _rms(om_ref[...], gm_ref[...]).astype(BF16)
    c = _rms(os_ref[...], gs_ref[...]).astype(BF16)
    y = _dot(a, w_ref[0:FOX_W, :]) + _dot(b, w_ref[FOX_W:FOX_W + MLA_W, :]) + _dot(c, w_ref[FOX_W + MLA_W:, :])
    o_ref[...] = h_ref[...] + y


def _out_proj(o_fox, o_mla, o_sb, h, lw, tm):
    n = h.shape[0]

    def row(w):
        return pl.BlockSpec((tm, w), lambda i: (i, 0))

    def full(a):
        return pl.BlockSpec(a.shape, lambda i: (0, 0))

    return pl.pallas_call(
        _out_kernel,
        grid=(n // tm,),
        in_specs=[row(FOX_W), row(MLA_W), row(SB_W), row(D_MODEL), full(lw["g_of"]), full(lw["g_om"]),
                  full(lw["g_os"]), full(lw["w_out"])],
        out_specs=row(D_MODEL),
        out_shape=jax.ShapeDtypeStruct((n, D_MODEL), F32),
        compiler_params=_cparams(("arbitrary",)),
        name="out_proj",
    )(o_fox, o_mla, o_sb, h, lw["g_of"], lw["g_om"], lw["g_os"], lw["w_out"])


def _memkv_kernel(x_ref, w_ref, k_ref, v_ref):
    y = _dot(x_ref[...].astype(BF16), w_ref[0])
    k_ref[0] = y[:, :MEM_W]
    v_ref[0] = y[:, MEM_W:]


def _mem_kv(mem, w_kv, tm):
    n = mem.shape[0]
    nl = w_kv.shape[0]
    out = jax.ShapeDtypeStruct((nl, n, MEM_W), F32)
    ospec = pl.BlockSpec((1, tm, MEM_W), lambda i, l: (l, i, 0))
    return pl.pallas_call(
        _memkv_kernel,
        grid=(n // tm, nl),
        in_specs=[pl.BlockSpec((tm, D_MODEL), lambda i, l: (i, 0)),
                  pl.BlockSpec((1, D_MODEL, 2 * MEM_W), lambda i, l: (l, 0, 0))],
        out_specs=[ospec, ospec],
        out_shape=[out, out],
        compiler_params=_cparams(("arbitrary", "arbitrary")),
        name="mem_kv",
    )(mem, w_kv)


def _mem_kernel(ns, tq, h_ref, g_ref, wq_ref, k_ref, v_ref, wo_ref, o_ref):
    h = h_ref[...]
    hn = _rms(h, g_ref[...]).astype(BF16)
    q = _dot(hn, wq_ref[...]) * MEM_SCALE
    outs = []
    for hd in range(MEM_HEADS):
        sl = slice(hd * MEM_HEAD_DIM, (hd + 1) * MEM_HEAD_DIM)
        qh = q[:, sl].reshape(ns, tq, MEM_HEAD_DIM).astype(BF16)
        kh = k_ref[:, :, sl].astype(BF16)
        vh = v_ref[:, :, sl].astype(BF16)
        s = jnp.einsum("sqd,skd->sqk", qh, kh, preferred_element_type=F32)
        m = jnp.max(s, axis=-1, keepdims=True)
        p = jnp.exp(s - m)
        p = p / jnp.sum(p, axis=-1, keepdims=True)
        o = jnp.einsum("sqk,skd->sqd", p.astype(BF16), vh, preferred_element_type=F32)
        outs.append(o.reshape(ns * tq, MEM_HEAD_DIM))
    o = jnp.concatenate(outs, axis=1).astype(BF16)
    o_ref[...] = h + _dot(o, wo_ref[...])


def _mem_attn(h, mem_k, mem_v, lw, ns, tq, per_seq):
    n = h.shape[0]
    tm = ns * tq
    assert per_seq % tq == 0 and (ns == 1 or tq == per_seq)
    spb = per_seq // tq
    mt = mem_k.shape[1]

    def kv_index(i):
        return (i // spb if ns == 1 else i, 0, 0)

    def full(a):
        return pl.BlockSpec(a.shape, lambda i: (0, 0))

    return pl.pallas_call(
        functools.partial(_mem_kernel, ns, tq),
        grid=(n // tm,),
        in_specs=[pl.BlockSpec((tm, D_MODEL), lambda i: (i, 0)), full(lw["g_mem"]), full(lw["w_mq"]),
                  pl.BlockSpec((ns, mt, MEM_W), kv_index), pl.BlockSpec((ns, mt, MEM_W), kv_index),
                  full(lw["w_mo"])],
        out_specs=pl.BlockSpec((tm, D_MODEL), lambda i: (i, 0)),
        out_shape=jax.ShapeDtypeStruct((n, D_MODEL), F32),
        compiler_params=_cparams(("arbitrary",)),
        name="mem_attn",
    )(h, lw["g_mem"], lw["w_mq"], mem_k, mem_v, lw["w_mo"])


def _mlp_kernel(final, h_ref, g_ref, wu_ref, wd_ref, gfin_ref, o_ref, hn_ref, acc_ref):
    j = pl.program_id(1)

    @pl.when(j == 0)
    def _():
        hn_ref[...] = _rms(h_ref[...], g_ref[...]).astype(BF16)
        acc_ref[...] = h_ref[...]

    u = jnp.maximum(_dot(hn_ref[...], wu_ref[...]), 0.0)
    acc_ref[...] += _dot((u * u).astype(BF16), wd_ref[...])

    @pl.when(j == pl.num_programs(1) - 1)
    def _():
        y = acc_ref[...]
        o_ref[...] = _rms(y, gfin_ref[...]) if final else y


def _mlp(h, lw, g_final, final, tm, tf):
    n = h.shape[0]
    return pl.pallas_call(
        functools.partial(_mlp_kernel, final),
        grid=(n // tm, D_FF // tf),
        in_specs=[pl.BlockSpec((tm, D_MODEL), lambda i, j: (i, 0)),
                  pl.BlockSpec((1, D_MODEL), lambda i, j: (0, 0)),
                  pl.BlockSpec((D_MODEL, tf), lambda i, j: (0, j)),
                  pl.BlockSpec((tf, D_MODEL), lambda i, j: (j, 0)),
                  pl.BlockSpec((1, D_MODEL), lambda i, j: (0, 0))],
        out_specs=pl.BlockSpec((tm, D_MODEL), lambda i, j: (i, 0)),
        out_shape=jax.ShapeDtypeStruct((n, D_MODEL), F32),
        scratch_shapes=[pltpu.VMEM((tm, D_MODEL), BF16), pltpu.VMEM((tm, D_MODEL), F32)],
        compiler_params=_cparams(("arbitrary", "arbitrary")),
        name="mlp",
    )(h, lw["g_mlp"], lw["w_up"], lw["w_down"], g_final)


def _swap_halves(w, width):
    lead = w.shape[:-1]
    g = w.reshape(lead + (-1, 2, width // 2))
    return g[..., ::-1, :].reshape(w.shape)


def _lane_blocks(w, width):
    d = w.shape[0]
    g = w.reshape(d, -1, width)
    return jnp.pad(g, ((0, 0), (0, 0), (0, LANE - width))).reshape(d, -1)


def _prep_layer(l, w_in, b_f, g_mix, g_q, w_uq, g_kv, w_uk, w_uv, g_of, g_om, g_os, w_out,
                g_mem, w_mq, w_mo, g_mlp, w_up, w_down):
    sizes = (FOX_W, KV_W, KV_W, FOX_HEADS, MLA_Q_RANK, MLA_KV_RANK, MLA_ROPE, SB_W, KV_W, KV_W)
    pts, acc = [], 0
    for s in sizes[:-1]:
        acc += s
        pts.append(acc)
    fq, fk, fv, ff, mqa, mkv, mkpe, sq, sk, sv = jnp.split(w_in[l], pts, axis=1)
    d = w_in.shape[1]

    def z(w):
        return jnp.zeros((d, w), F32)

    w_in_r = jnp.concatenate(
        [fk, fv, sk, sv, mkv, mkpe, z(LANE - MLA_ROPE), _swap_halves(mkpe, MLA_ROPE), z(LANE - MLA_ROPE),
         ff, z(LANE - FOX_HEADS), mqa, z(2 * LANE - MLA_Q_RANK),
         _lane_blocks(fq, HEAD_DIM), _lane_blocks(fk, HEAD_DIM), _lane_blocks(fv, HEAD_DIM),
         _lane_blocks(sq, HEAD_DIM), _lane_blocks(sk, HEAD_DIM), _lane_blocks(sv, HEAD_DIM)], axis=1)
    assert w_in_r.shape[1] == N_INR
    uq = w_uq[l].reshape(MLA_Q_RANK, MLA_HEADS, MLA_NOPE + MLA_ROPE)
    nope = uq[:, :, :MLA_NOPE].reshape(MLA_Q_RANK, MLA_HEADS * MLA_NOPE)
    pe = uq[:, :, MLA_NOPE:].reshape(MLA_Q_RANK, MLA_HEADS * MLA_ROPE)
    w_uq_r = jnp.concatenate([nope, _lane_blocks(pe, MLA_ROPE), _lane_blocks(_swap_halves(pe, MLA_ROPE), MLA_ROPE)],
                             axis=1)
    assert w_uq_r.shape[1] == N_UQ
    w_uq_r = jnp.pad(w_uq_r, ((0, 2 * LANE - MLA_Q_RANK), (0, 0)))
    w_uk_r = jnp.pad(w_uk[l].transpose(1, 2, 0), ((0, 0), (0, 0), (0, 0)))
    return dict(
        g_mix=g_mix[l][None, :],
        w_in=w_in_r.astype(BF16),
        w_fft=jnp.pad(ff.T, ((0, 8 - FOX_HEADS), (0, 0))).astype(BF16),
        b_f=jnp.pad(b_f[l], (0, LANE - FOX_HEADS))[None, :],
        b_ft=jnp.pad(b_f[l], (0, 8 - FOX_HEADS))[:, None],
        g_q=jnp.pad(g_q[l], (0, 2 * LANE - MLA_Q_RANK))[None, :],
        w_uq=w_uq_r.astype(BF16),
        g_kv=g_kv[l][None, :],
        w_uk=w_uk_r.astype(BF16),
        w_uv=w_uv[l].transpose(1, 0, 2).astype(BF16),
        g_of=g_of[l][None, :], g_om=g_om[l][None, :], g_os=g_os[l][None, :],
        w_out=w_out[l].astype(BF16),
        g_mem=g_mem[l][None, :], w_mq=w_mq[l].astype(BF16), w_mo=w_mo[l].astype(BF16),
        g_mlp=g_mlp[l][None, :], w_up=w_up[l].astype(BF16), w_down=w_down[l].astype(BF16),
    )


def _constants():
    nblk = FOX_HEADS + 2 * FOX_KV_HEADS
    sel = np.zeros((3 * LANE, nblk * LANE), np.float32)
    ones = np.zeros((1, nblk * LANE), np.float32)
    for h in range(FOX_HEADS):
        hk, g = divmod(h, GROUP)
        for j in range(3):
            sel[j * LANE + h, h * LANE + X_C + j] = 1.0
            ones[0, h * LANE + X_G + 3 * g + j] = 1.0
            sel[j * LANE + h, (FOX_HEADS + hk) * LANE + X_G + 3 * g + j] = -1.0
    for hk in range(FOX_KV_HEADS):
        ones[0, (FOX_HEADS + hk) * LANE + X_C:(FOX_HEADS + hk) * LANE + X_C + 3] = 1.0
        ones[0, (FOX_HEADS + FOX_KV_HEADS + hk) * LANE + X_ONE_V] = 1.0
    idx = np.arange(MXU)
    later = (idx[:, None] > idx[None, :]).astype(np.float32)
    return dict(sel=jnp.asarray(sel, BF16), ones=jnp.asarray(ones, F32), later=jnp.asarray(later, BF16))


def _rope_tables(pos):
    half = MLA_ROPE // 2
    inv_freq = jnp.power(ROPE_BASE, -jnp.arange(half, dtype=F32) / half)
    ang = pos.astype(F32)[:, None] * inv_freq[None, :]
    cos, sin = jnp.cos(ang), jnp.sin(ang)
    pad = ((0, 0), (0, LANE - MLA_ROPE))
    return jnp.pad(jnp.concatenate([cos, cos], axis=1), pad), jnp.pad(jnp.concatenate([-sin, sin], axis=1), pad)


TM_IN = 512
TB_ATT = 512
TM_MLP = 1024
TF_MLP = 1024
TQ_MEM = 512
NS_MEM = 16
PP = 16


def kernel(x_prompt, x_sample, cache_fox_k, cache_fox_v, cache_fox_logf, cache_mla_ckv, cache_mla_kpe,
           cache_sb_k, cache_sb_v, cache_mem_k, cache_mem_v, page_table, mem_prompt,
           g_mix, w_in, b_fox_f, g_mla_q, w_mla_uq, g_mla_kv, w_mla_uk, w_mla_uv,
           g_out_fox, g_out_mla, g_out_sb, w_out, g_mem, w_mem_q, w_mem_k, w_mem_v, w_mem_o,
           g_mlp, w_up, w_down, g_final):
    bsz, seq, _ = x_prompt.shape
    s_cnt, n_new, _ = x_sample.shape
    depth = w_in.shape[0]
    past_len = page_table.shape[1] * PAGE
    assert n_new == N_NEW
    n_p, n_s = bsz * seq, s_cnt * n_new
    consts = _constants()

    cos_p, sin_p = _rope_tables(jnp.arange(seq, dtype=jnp.int32))
    cos_p, sin_p = jnp.tile(cos_p, (bsz, 1)), jnp.tile(sin_p, (bsz, 1))
    cos_s, sin_s = _rope_tables(past_len + jnp.arange(n_new, dtype=jnp.int32))
    cos_s, sin_s = jnp.tile(cos_s, (s_cnt, 1)), jnp.tile(sin_s, (s_cnt, 1))

    pool = cache_fox_k.shape[1]

    def keys_minor(c):
        return c.transpose(0, 1, 3, 4, 2).reshape(depth, pool, KV_W, PAGE)

    caches = (keys_minor(cache_fox_k), keys_minor(cache_fox_v), cache_fox_logf.transpose(0, 3, 1, 2),
              cache_mla_ckv, cache_mla_kpe.transpose(0, 1, 3, 2), keys_minor(cache_sb_k), keys_minor(cache_sb_v))
    mem_tok = mem_prompt.shape[1]
    w_kv = jnp.concatenate([w_mem_k, w_mem_v], axis=2).astype(BF16)
    p_mem_k, p_mem_v = _mem_kv(mem_prompt.reshape(bsz * mem_tok, D_MODEL), w_kv, 512)
    s_mem_k = cache_mem_k.reshape(depth, s_cnt, mem_tok, MEM_W)
    s_mem_v = cache_mem_v.reshape(depth, s_cnt, mem_tok, MEM_W)
    g_fin = g_final[None, :]

    hp = x_prompt.reshape(n_p, D_MODEL)
    hs = x_sample.reshape(n_s, D_MODEL)
    rows_p, rows_s = [], []
    tm_p, tm_s = min(TM_IN, seq), min(TM_IN, n_s)
    tmlp_p, tmlp_s = min(TM_MLP, n_p), min(TM_MLP, n_s)
    tb = min(TB_ATT, seq)
    tq_mem = min(TQ_MEM, seq)
    ns_mem = min(NS_MEM, s_cnt)
    pp = min(PP, page_table.shape[1])
    for l in range(depth):
        lw = _prep_layer(l, w_in, b_fox_f, g_mix, g_mla_q, w_mla_uq, g_mla_kv, w_mla_uk, w_mla_uv,
                         g_out_fox, g_out_mla, g_out_sb, w_out, g_mem, w_mem_q, w_mem_o, g_mlp, w_up, w_down)
        final = l == depth - 1

        r = _in_proj(hp, lw, consts, cos_p, sin_p, seq, tm_p)
        o_fox, o_mla, o_sb = _prompt_attn(bsz, seq, tb, r, consts["later"], lw["w_uv"])
        hp = _out_proj(o_fox, o_mla, o_sb, hp, lw, tm_p)
        hp = _mem_attn(hp, p_mem_k[l].reshape(bsz, mem_tok, MEM_W), p_mem_v[l].reshape(bsz, mem_tok, MEM_W),
                       lw, 1, tq_mem, seq)
        hp = _mlp(hp, lw, g_fin, final, tmlp_p, TF_MLP)
        rows_p.append(r)

        r = _in_proj(hs, lw, consts, cos_s, sin_s, n_new, tm_s)
        qops, new_ops = _sample_operands(r, s_cnt)
        ofs, om = _sample_attn(l, page_table, qops, new_ops, consts["later"], caches, pp)
        o_fox, o_lat, o_sb = _sample_unpack(ofs, om, s_cnt)
        o_mla = _uv_proj(o_lat, lw["w_uv"])
        hs = _out_proj(o_fox, o_mla, o_sb, hs, lw, tm_s)
        hs = _mem_attn(hs, s_mem_k[l], s_mem_v[l], lw, ns_mem, n_new, n_new)
        hs = _mlp(hs, lw, g_fin, final, tmlp_s, TF_MLP)
        rows_s.append(r)

    def stack(rows, key, shape):
        return jnp.stack([r[key] for r in rows]).reshape((depth,) + shape)

    def group_rows(rows, lead):
        return (stack(rows, "fk32", lead + (FOX_KV_HEADS, HEAD_DIM)), stack(rows, "fv32", lead + (FOX_KV_HEADS, HEAD_DIM)),
                stack(rows, "logf", lead + (FOX_HEADS,)), stack(rows, "ckv32", lead + (MLA_KV_RANK,)),
                stack(rows, "kpe32", lead + (MLA_ROPE,)),
                stack(rows, "sk32", lead + (SB_KV_HEADS, HEAD_DIM)), stack(rows, "sv32", lead + (SB_KV_HEADS, HEAD_DIM)))

    y_prompt = hp.reshape(bsz, seq, D_MODEL)
    y_sample = hs.reshape(s_cnt, n_new, D_MODEL)
    pm_shape = (depth, bsz, mem_tok, MEM_HEADS, MEM_HEAD_DIM)
    return ((y_prompt, y_sample) + group_rows(rows_p, (bsz, seq))
            + (p_mem_k.reshape(pm_shape), p_mem_v.reshape(pm_shape)) + group_rows(rows_s, (s_cnt, n_new)))
```

```python
import functools
import math

import jax
import jax.numpy as jnp
import numpy as np
from jax import lax
from jax.experimental import pallas as pl
from jax.experimental.pallas import tpu as pltpu

F32 = jnp.float32
BF16 = jnp.bfloat16

D_MODEL = 1024
HEAD_DIM = 64
FOX_HEADS = 6
FOX_KV_HEADS = 2
SB_HEADS = 6
SB_KV_HEADS = 2
GROUP = 3
MLA_HEADS = 4
MLA_Q_RANK = 192
MLA_KV_RANK = 128
MLA_NOPE = 64
MLA_ROPE = 32
MLA_V = 64
ROPE_BASE = 10000.0
MEM_HEADS = 4
MEM_HEAD_DIM = 64
MEM_W = MEM_HEADS * MEM_HEAD_DIM
D_FF = 4 * D_MODEL
EPS = 1e-6
PAGE = 128
N_NEW = 8
FOX_W = FOX_HEADS * HEAD_DIM
MLA_W = MLA_HEADS * MLA_V
SB_W = SB_HEADS * HEAD_DIM
KV_W = FOX_KV_HEADS * HEAD_DIM
ATT_SCALE = 1.0 / math.sqrt(HEAD_DIM)
MLA_SCALE = 1.0 / math.sqrt(MLA_NOPE + MLA_ROPE)
MEM_SCALE = 1.0 / math.sqrt(MEM_HEAD_DIM)
LOG2E = math.log2(math.e)
MLA_EXP2 = MLA_SCALE * LOG2E

LANE = 128
MXU = 256
VMEM_LIMIT = 48 * 1024 * 1024

X_C = HEAD_DIM
X_G = HEAD_DIM + 3
X_ONE_V = HEAD_DIM
X_ONE_M = MLA_KV_RANK + MLA_ROPE

C_FK = 0
C_FV = C_FK + KV_W
C_SK = C_FV + KV_W
C_SV = C_SK + KV_W
C_CKV = C_SV + KV_W
C_KPE = C_CKV + MLA_KV_RANK
C_KPS = C_KPE + LANE
C_FF = C_KPS + LANE
C_MQA = C_FF + LANE
C_PQF = C_MQA + 2 * LANE
C_PKF = C_PQF + FOX_HEADS * LANE
C_PVF = C_PKF + FOX_KV_HEADS * LANE
C_PQS = C_PVF + FOX_KV_HEADS * LANE
C_PKS = C_PQS + SB_HEADS * LANE
C_PVS = C_PKS + SB_KV_HEADS * LANE
N_INR = C_PVS + SB_KV_HEADS * LANE

U_NOPE = 0
U_PE = MLA_HEADS * MLA_NOPE
U_PS = U_PE + MLA_HEADS * LANE
N_UQ = U_PS + MLA_HEADS * LANE

NEG_INF = float("-inf")


def _cparams(sem):
    return pltpu.CompilerParams(dimension_semantics=sem, vmem_limit_bytes=VMEM_LIMIT)


def _log_sigmoid(x):
    return jnp.minimum(x, 0.0) - jnp.log(1.0 + jnp.exp(-jnp.abs(x)))


def _split3(x):
    hi = x.astype(BF16)
    r = x - hi.astype(F32)
    mid = r.astype(BF16)
    lo = (r - mid.astype(F32)).astype(BF16)
    return hi, mid, lo


def _split2(x):
    hi = x.astype(BF16)
    lo = (x - hi.astype(F32)).astype(BF16)
    return hi, lo


def _dot(a, b):
    return jnp.dot(a, b, preferred_element_type=F32)


def _dot_nt(a, b):
    return lax.dot_general(a, b, (((1,), (1,)), ((), ())), preferred_element_type=F32)


def _rms(x, g, n=None):
    n = x.shape[-1] if n is None else n
    ms = jnp.sum(x * x, axis=-1, keepdims=True) * (1.0 / n)
    return x * lax.rsqrt(ms + EPS) * g


def _in_kernel(seg, tm, h_ref, g_ref, w_ref, wfft_ref, bf_ref, bft_ref, gq_ref, wuq_ref, gkv_ref,
               wuk_ref, cos_ref, sin_ref, sel_ref, one_ref,
               fk32, fv32, sk32, sv32, ckv32, kpe32, logf_o, logft_o,
               qf_o, kf_o, vf_o, qs_o, ks_o, vs_o, qm_o, kvm_o, carry_ref):
    i = pl.program_id(0)
    xn = _rms(h_ref[...], g_ref[...]).astype(BF16)
    proj = _dot(xn, w_ref[...])

    fk32[...] = proj[:, C_FK:C_FK + KV_W]
    fv32[...] = proj[:, C_FV:C_FV + KV_W]
    sk32[...] = proj[:, C_SK:C_SK + KV_W]
    sv32[...] = proj[:, C_SV:C_SV + KV_W]

    logf = _log_sigmoid(proj[:, C_FF:C_FF + LANE] + bf_ref[...])
    logf_o[...] = logf[:, :FOX_HEADS]
    logft_o[...] = _log_sigmoid(_dot_nt(wfft_ref[...], xn) + bft_ref[...])
    r = lax.broadcasted_iota(jnp.int32, (tm, tm), 0)
    cidx = lax.broadcasted_iota(jnp.int32, (tm, tm), 1)
    lower = cidx <= r
    if seg < tm:
        lower = lower & ((r // seg) == (cidx // seg))
    lower = jnp.where(lower, 1.0, 0.0).astype(BF16)
    hi, mid, lo = _split3(logf)
    c3 = _dot(lower, jnp.concatenate([hi, mid, lo], axis=1))
    c = (c3[:, :LANE] + c3[:, LANE:2 * LANE]) + c3[:, 2 * LANE:]
    if seg > tm:
        @pl.when(i % (seg // tm) == 0)
        def _():
            carry_ref[...] = jnp.zeros_like(carry_ref)

        c = c + carry_ref[...]
        carry_ref[...] = c[tm - 1:tm, :]
    hi, mid, lo = _split3(c)
    ext = _dot(jnp.concatenate([hi, mid, lo], axis=1), sel_ref[...]) + one_ref[...]
    nq = FOX_HEADS * LANE
    nk = FOX_KV_HEADS * LANE
    qf_o[...] = (proj[:, C_PQF:C_PQF + nq] * ATT_SCALE + ext[:, :nq]).astype(BF16)
    kf_o[...] = (proj[:, C_PKF:C_PKF + nk] + ext[:, nq:nq + nk]).astype(BF16)
    vf_o[...] = (proj[:, C_PVF:C_PVF + nk] + ext[:, nq + nk:]).astype(BF16)
    qs_o[...] = (proj[:, C_PQS:C_PQS + nq] * ATT_SCALE).astype(BF16)
    ks_o[...] = proj[:, C_PKS:C_PKS + nk].astype(BF16)
    vs_o[...] = proj[:, C_PVS:C_PVS + nk].astype(BF16)

    ckv = _rms(proj[:, C_CKV:C_CKV + MLA_KV_RANK], gkv_ref[...])
    ckv32[...] = ckv
    cos = cos_ref[...]
    sin = sin_ref[...]
    kpe = proj[:, C_KPE:C_KPE + LANE] * cos + proj[:, C_KPS:C_KPS + LANE] * sin
    kpe32[...] = kpe[:, :MLA_ROPE]
    lane = lax.broadcasted_iota(jnp.int32, (1, LANE), 1)
    kvm_o[:, :LANE] = ckv.astype(BF16)
    kvm_o[:, LANE:] = (kpe + jnp.where(lane == MLA_ROPE, 1.0, 0.0)).astype(BF16)
    qa = _rms(proj[:, C_MQA:C_MQA + 2 * LANE], gq_ref[...], MLA_Q_RANK).astype(BF16)
    q = _dot(qa, wuq_ref[...])
    for hh in range(MLA_HEADS):
        qn = q[:, U_NOPE + hh * MLA_NOPE:U_NOPE + (hh + 1) * MLA_NOPE].astype(BF16)
        qm_o[:, hh * MXU:hh * MXU + LANE] = _dot(qn, wuk_ref[hh]).astype(BF16)
        qpe = (q[:, U_PE + hh * LANE:U_PE + (hh + 1) * LANE] * cos
               + q[:, U_PS + hh * LANE:U_PS + (hh + 1) * LANE] * sin)
        qm_o[:, hh * MXU + LANE:(hh + 1) * MXU] = qpe.astype(BF16)


def _in_proj(h, lw, consts, cos, sin, seg, tm):
    n = h.shape[0]
    assert n % tm == 0 and (seg % tm == 0 or tm % seg == 0)

    def row(w):
        return pl.BlockSpec((tm, w), lambda i: (i, 0))

    def full(a):
        nd = a.ndim
        return pl.BlockSpec(a.shape, lambda i, _nd=nd: (0,) * _nd)

    ins = [h, lw["g_mix"], lw["w_in"], lw["w_fft"], lw["b_f"], lw["b_ft"], lw["g_q"], lw["w_uq"],
           lw["g_kv"], lw["w_uk"], cos, sin, consts["sel"], consts["ones"]]
    in_specs = [row(D_MODEL)] + [full(a) for a in ins[1:10]] + [row(LANE), row(LANE), full(ins[12]), full(ins[13])]
    outs = [
        ("fk32", KV_W, F32), ("fv32", KV_W, F32), ("sk32", KV_W, F32), ("sv32", KV_W, F32),
        ("ckv32", MLA_KV_RANK, F32), ("kpe32", MLA_ROPE, F32), ("logf", FOX_HEADS, F32), ("logft", None, F32),
        ("qf", FOX_HEADS * LANE, BF16), ("kf", FOX_KV_HEADS * LANE, BF16), ("vf", FOX_KV_HEADS * LANE, BF16),
        ("qs", SB_HEADS * LANE, BF16), ("ks", SB_KV_HEADS * LANE, BF16), ("vs", SB_KV_HEADS * LANE, BF16),
        ("qm", MLA_HEADS * MXU, BF16), ("kvm", MXU, BF16),
    ]
    out_shape, out_specs = [], []
    for _, w, dt in outs:
        if w is None:
            out_shape.append(jax.ShapeDtypeStruct((8, n), dt))
            out_specs.append(pl.BlockSpec((8, tm), lambda i: (0, i)))
        else:
            out_shape.append(jax.ShapeDtypeStruct((n, w), dt))
            out_specs.append(row(w))
    res = pl.pallas_call(
        functools.partial(_in_kernel, seg, tm),
        grid=(n // tm,),
        in_specs=in_specs,
        out_specs=out_specs,
        out_shape=out_shape,
        scratch_shapes=[pltpu.VMEM((1, LANE), F32)],
        compiler_params=_cparams(("arbitrary",)),
        name="in_proj",
    )(*ins)
    return dict(zip([o[0] for o in outs], res))


def _pair_tables(nq):
    qi, kj = [], []
    for i in range(nq):
        for j in range(i, -1, -1):
            qi.append(i)
            kj.append(j)
    return jnp.asarray(qi, jnp.int32), jnp.asarray(kj, jnp.int32)


def _heads_on_rows(ref, first, count, width):
    return jnp.concatenate([ref[:, (first + g) * width:(first + g + 1) * width] for g in range(count)], axis=0)


def _attn_kernel(tb, qi_ref, kj_ref, qf_ref, kf_ref, vf_ref, qs_ref, ks_ref, vs_ref, qm_ref, kvm_ref,
                 later_ref, wuv_ref, of_ref, om_ref, os_ref,
                 fm_ref, facc_ref, mm_ref, macc_ref, sacc_ref, scar_ref):
    p = pl.program_id(1)
    qi = qi_ref[p]
    kj = kj_ref[p]

    @pl.when(kj == qi)
    def _():
        fm_ref[...] = jnp.full_like(fm_ref, NEG_INF)
        mm_ref[...] = jnp.full_like(mm_ref, NEG_INF)
        facc_ref[...] = jnp.zeros_like(facc_ref)
        macc_ref[...] = jnp.zeros_like(macc_ref)
        sacc_ref[...] = jnp.zeros_like(sacc_ref)
        scar_ref[...] = jnp.zeros_like(scar_ref)

    def body(diag):
        def masks(rows):
            r = lax.broadcasted_iota(jnp.int32, (rows, tb), 0) % tb
            c = lax.broadcasted_iota(jnp.int32, (rows, tb), 1)
            return c <= r, c < r

        if diag:
            allow_g, strict_g = masks(GROUP * tb)
            allow_m, _ = masks(MLA_HEADS * tb)

        for hk in range(FOX_KV_HEADS):
            q = _heads_on_rows(qf_ref, hk * GROUP, GROUP, LANE)
            s = _dot_nt(q, kf_ref[:, hk * LANE:(hk + 1) * LANE])
            if diag:
                s = jnp.where(allow_g, s, NEG_INF)
            m_prev = fm_ref[hk]
            m_new = jnp.maximum(m_prev, jnp.max(s, axis=1, keepdims=True))
            pm = jnp.exp(s - m_new).astype(BF16)
            facc_ref[hk] = jnp.exp(m_prev - m_new) * facc_ref[hk] + _dot(pm, vf_ref[:, hk * LANE:(hk + 1) * LANE])
            fm_ref[hk] = m_new

        q = _heads_on_rows(qm_ref, 0, MLA_HEADS, MXU)
        kv = kvm_ref[...]
        s = _dot_nt(q, kv)
        if diag:
            s = jnp.where(allow_m, s, NEG_INF)
        m_prev = mm_ref[...]
        m_new = jnp.maximum(m_prev, jnp.max(s, axis=1, keepdims=True))
        pm = jnp.exp2((s - m_new) * MLA_EXP2).astype(BF16)
        macc_ref[...] = jnp.exp2((m_prev - m_new) * MLA_EXP2) * macc_ref[...] + _dot(pm, kv)
        mm_ref[...] = m_new

        later = later_ref[...]
        rows = GROUP * tb
        for hk in range(SB_KV_HEADS):
            q = _heads_on_rows(qs_ref, hk * GROUP, GROUP, LANE)
            z = _dot_nt(q, ks_ref[:, hk * LANE:(hk + 1) * LANE]) * LOG2E
            ls = jnp.minimum(z, 0.0) - jnp.log2(1.0 + jnp.exp2(-jnp.abs(z)))
            lk = ls - z
            if diag:
                lk = jnp.where(strict_g, lk, 0.0)
            hi, lo = _split2(lk)
            hl = jnp.concatenate([hi, lo], axis=0)
            car = scar_ref[hk]
            parts = [None] * (tb // MXU)
            for t in range(tb // MXU - 1, -1, -1):
                sf = _dot(hl[:, t * MXU:(t + 1) * MXU], later)
                sf = sf[:rows] + sf[rows:]
                parts[t] = sf + car
                car = car + (sf[:, 0:1] + lk[:, t * MXU:t * MXU + 1])
            scar_ref[hk] = car
            sfx = parts[0] if len(parts) == 1 else jnp.concatenate(parts, axis=1)
            a = jnp.exp2(ls + sfx)
            if diag:
                a = jnp.where(strict_g, a, 0.0)
            sacc_ref[hk] = sacc_ref[hk] + _dot(a.astype(BF16), vs_ref[:, hk * LANE:(hk + 1) * LANE])

    @pl.when(kj == qi)
    def _():
        body(True)

    @pl.when(kj < qi)
    def _():
        body(False)

    @pl.when(kj == 0)
    def _():
        for hk in range(FOX_KV_HEADS):
            for g in range(GROUP):
                h = hk * GROUP + g
                rows = slice(g * tb, (g + 1) * tb)
                of_ref[:, h * HEAD_DIM:(h + 1) * HEAD_DIM] = (
                    facc_ref[hk, rows, 0:HEAD_DIM] / facc_ref[hk, rows, X_ONE_V:X_ONE_V + 1])
                os_ref[:, h * HEAD_DIM:(h + 1) * HEAD_DIM] = sacc_ref[hk, rows, 0:HEAD_DIM]
        for h in range(MLA_HEADS):
            rows = slice(h * tb, (h + 1) * tb)
            o_lat = (macc_ref[rows, 0:MLA_KV_RANK] / macc_ref[rows, X_ONE_M:X_ONE_M + 1]).astype(BF16)
            om_ref[:, h * MLA_V:(h + 1) * MLA_V] = _dot(o_lat, wuv_ref[h])


def _prompt_attn(bsz, seq, tb, r, later, wuv):
    nq = seq // tb
    qi_t, kj_t = _pair_tables(nq)
    n = bsz * seq

    def qspec(w):
        return pl.BlockSpec((tb, w), lambda b, p, qi, kj: (b * nq + qi[p], 0))

    def kspec(w):
        return pl.BlockSpec((tb, w), lambda b, p, qi, kj: (b * nq + kj[p], 0))

    nq_l, nk_l = FOX_HEADS * LANE, FOX_KV_HEADS * LANE
    in_specs = [qspec(nq_l), kspec(nk_l), kspec(nk_l), qspec(nq_l), kspec(nk_l), kspec(nk_l),
                qspec(MLA_HEADS * MXU), kspec(MXU),
                pl.BlockSpec(later.shape, lambda b, p, qi, kj: (0, 0)),
                pl.BlockSpec(wuv.shape, lambda b, p, qi, kj: (0, 0, 0))]
    scratch = [pltpu.VMEM((FOX_KV_HEADS, GROUP * tb, 1), F32), pltpu.VMEM((FOX_KV_HEADS, GROUP * tb, LANE), F32),
               pltpu.VMEM((MLA_HEADS * tb, 1), F32), pltpu.VMEM((MLA_HEADS * tb, MXU), F32),
               pltpu.VMEM((SB_KV_HEADS, GROUP * tb, LANE), F32), pltpu.VMEM((SB_KV_HEADS, GROUP * tb, 1), F32)]
    return pl.pallas_call(
        functools.partial(_attn_kernel, tb),
        grid_spec=pltpu.PrefetchScalarGridSpec(
            num_scalar_prefetch=2,
            grid=(bsz, int(qi_t.shape[0])),
            in_specs=in_specs,
            out_specs=[qspec(FOX_W), qspec(MLA_W), qspec(SB_W)],
            scratch_shapes=scratch),
        out_shape=[jax.ShapeDtypeStruct((n, FOX_W), F32), jax.ShapeDtypeStruct((n, MLA_W), F32),
                   jax.ShapeDtypeStruct((n, SB_W), F32)],
        compiler_params=_cparams(("arbitrary", "arbitrary")),
        name="prompt_attn",
    )(qi_t, kj_t, r["qf"], r["kf"], r["vf"], r["qs"], r["ks"], r["vs"], r["qm"], r["kvm"], later, wuv)


R_FOX = FOX_HEADS * N_NEW
R_FS = R_FOX + SB_HEADS * N_NEW
R_MLA = MLA_HEADS * N_NEW
R_CUM = SB_HEADS * N_NEW + 8
N_ARR = 7
SUB_PAGES = 16


def _sample_kernel(pp, nch, layer, pt_ref, qfs_ref, qm1_ref, qm2_ref, nk_ref, nv_ref, nckv_ref, nkpe_ref, nlf_ref,
                   later_ref, *rest):
    caches = rest[:N_ARR]
    ofs_ref, om_ref = rest[N_ARR:N_ARR + 2]
    mf_ref, lf_ref, mm_ref, lm_ref, accfs_ref, accm_ref, carry_ref = rest[N_ARR + 2:N_ARR + 9]
    bufs = rest[N_ARR + 9:2 * N_ARR + 9]
    sem = rest[2 * N_ARR + 9]
    b = pl.program_id(0)
    cstep = pl.program_id(1)
    step = b * nch + cstep
    slot = step % 2
    qfs = qfs_ref[0]
    qm1 = qm1_ref[0]
    qm2 = qm2_ref[0]

    nsub = pp // SUB_PAGES

    def sub_copies(bb, cc, sl, s):
        out = []
        for i in range(s * SUB_PAGES, (s + 1) * SUB_PAGES):
            pg = pt_ref[bb, (nch - 1 - cc) * pp + i]
            for a in range(N_ARR):
                if a == 2:
                    src = caches[a].at[layer, :, pl.ds(pl.multiple_of((pg // 8) * 8, 8), 8), :]
                else:
                    src = caches[a].at[layer, pg]
                out.append(pltpu.make_async_copy(src, bufs[a].at[sl, i], sem.at[sl, s, a]))
        return out

    def start_all(copies):
        for k, cp in enumerate(copies):
            cp.start(priority=k % 2)

    @pl.when(step == 0)
    def _():
        for s in range(nsub - 1, -1, -1):
            start_all(sub_copies(b, cstep, slot, s))

    last_c = cstep == nch - 1
    last_step = step == pl.num_programs(0) * nch - 1
    nb = jnp.where(last_c & jnp.logical_not(last_step), b + 1, b)
    nc = jnp.where(last_step, cstep, jnp.where(last_c, 0, cstep + 1))
    next_copies = [cp for s in range(nsub - 1, -1, -1) for cp in sub_copies(nb, nc, 1 - slot, s)]
    n_issue = 4

    def issue_now(k):
        for j in range(k * len(next_copies) // n_issue, (k + 1) * len(next_copies) // n_issue):
            next_copies[j].start(priority=j % 2)

    def issue(k):
        pl.when(step >= 0)(lambda: issue_now(k))

    issue_now(0)

    def block(s_fs, s_m, lf8, width, new, pv_fs, pv_m, hook=lambda k: None):
        z = s_fs[R_FOX:]
        ls = jnp.minimum(z, 0.0) - jnp.log(1.0 + jnp.exp(-jnp.abs(z)))
        lk = ls - z
        if new:
            key = lax.broadcasted_iota(jnp.int32, (R_FOX, width), 1)
            tok = lax.broadcasted_iota(jnp.int32, (R_FOX, width), 0) % N_NEW
            strict = key < tok
            allow = key <= tok
            keym = lax.broadcasted_iota(jnp.int32, (R_MLA, width), 1)
            allow_m = keym <= (lax.broadcasted_iota(jnp.int32, (R_MLA, width), 0) % N_NEW)
            lk = jnp.where(strict, lk, 0.0)
        y = jnp.concatenate([lk, lf8], axis=0)
        hi, lo = _split2(y)
        hl = jnp.concatenate([hi, lo], axis=0)
        tile = min(width, MXU)
        nt = width // tile
        later = later_ref[0:tile, 0:tile]
        sfs, tots = [], []
        for t in range(nt):
            sf = _dot(hl[:, t * tile:(t + 1) * tile], later)
            sf = sf[:R_CUM] + sf[R_CUM:]
            sfs.append(sf)
            tots.append(sf[:, 0:1] + y[:, t * tile:t * tile + 1])
        car = carry_ref[...]
        parts = [None] * nt
        for t in range(nt - 1, -1, -1):
            parts[t] = sfs[t] + car
            car = car + tots[t]
        carry_ref[...] = car
        hook(1)
        sfx = parts[0] if nt == 1 else jnp.concatenate(parts, axis=1)
        decay = jnp.concatenate(
            [jnp.broadcast_to(sfx[R_CUM - 8 + h:R_CUM - 7 + h], (N_NEW, width)) for h in range(FOX_HEADS)], axis=0)
        tf = s_fs[:R_FOX] + decay
        a = jnp.exp(ls + sfx[:R_CUM - 8])
        if new:
            tf = jnp.where(allow, tf, NEG_INF)
            s_m = jnp.where(allow_m, s_m, NEG_INF)
            a = jnp.where(strict, a, 0.0)
        m_prev = mf_ref[...]
        m_new = jnp.maximum(m_prev, jnp.max(tf, axis=1, keepdims=True))
        alpha = jnp.exp(m_prev - m_new)
        pf = jnp.exp(tf - m_new)
        lf_ref[...] = alpha * lf_ref[...] + jnp.sum(pf, axis=1, keepdims=True)
        mf_ref[...] = m_new
        pfs = jnp.concatenate([pf, a], axis=0).astype(BF16)
        hook(2)
        o = pv_fs(pfs)
        accfs_ref[0:R_FOX, :] = alpha * accfs_ref[0:R_FOX, :] + o[:R_FOX]
        accfs_ref[R_FOX:, :] = accfs_ref[R_FOX:, :] + o[R_FOX:]
        hook(3)
        m_prev = mm_ref[...]
        m_new = jnp.maximum(m_prev, jnp.max(s_m, axis=1, keepdims=True))
        alpha = jnp.exp2((m_prev - m_new) * MLA_EXP2)
        pm = jnp.exp2((s_m - m_new) * MLA_EXP2)
        lm_ref[...] = alpha * lm_ref[...] + jnp.sum(pm, axis=1, keepdims=True)
        mm_ref[...] = m_new
        accm_ref[...] = alpha * accm_ref[...] + pv_m(pm.astype(BF16))

    @pl.when(cstep == 0)
    def _():
        mf_ref[...] = jnp.full_like(mf_ref, NEG_INF)
        mm_ref[...] = jnp.full_like(mm_ref, NEG_INF)
        lf_ref[...] = jnp.zeros_like(lf_ref)
        lm_ref[...] = jnp.zeros_like(lm_ref)
        accfs_ref[...] = jnp.zeros_like(accfs_ref)
        accm_ref[...] = jnp.zeros_like(accm_ref)
        carry_ref[...] = jnp.zeros_like(carry_ref)
        nk = nk_ref[0].astype(BF16)
        nv = nv_ref[0].astype(BF16)
        nckv = nckv_ref[0].astype(BF16)
        nkpe = nkpe_ref[0].astype(BF16)
        block(_dot_nt(qfs, nk), _dot_nt(qm1, nckv) + _dot_nt(qm2, nkpe), nlf_ref[0], PAGE, True,
              lambda pfs: _dot(pfs, nv), lambda pm: _dot(pm, nckv))

    for s in range(nsub - 1, -1, -1):
        for cp in sub_copies(b, cstep, slot, s):
            cp.wait()
        pages = range(s * SUB_PAGES, (s + 1) * SUB_PAGES)

        def cat(a, axis, pages=pages):
            return jnp.concatenate([bufs[a][slot, i].astype(BF16) for i in pages], axis=axis)

        kfs = jnp.concatenate([cat(0, 1), cat(5, 1)], axis=0)
        vfs = jnp.concatenate([cat(1, 1), cat(6, 1)], axis=0)
        ckv = cat(3, 0)
        kpet = cat(4, 1)
        lfs = []
        for i in pages:
            row = pt_ref[b, (nch - 1 - cstep) * pp + i] % 8
            six = jnp.concatenate([bufs[2][slot, i, h, pl.ds(row, 1), :] for h in range(FOX_HEADS)], axis=0)
            lfs.append(jnp.concatenate([six, jnp.zeros((8 - FOX_HEADS, PAGE), F32)], axis=0))
        lf8 = jnp.concatenate(lfs, axis=1)
        block(_dot(qfs, kfs), _dot_nt(qm1, ckv) + _dot(qm2, kpet), lf8, SUB_PAGES * PAGE, False,
              lambda pfs, vfs=vfs: _dot_nt(pfs, vfs), lambda pm, ckv=ckv: _dot(pm, ckv),
              hook=issue if s == nsub - 1 else (lambda k: None))

    @pl.when(cstep == nch - 1)
    def _():
        inv = 1.0 / lf_ref[...]
        ofs_ref[0, 0:R_FOX, :] = accfs_ref[0:R_FOX, :] * inv
        ofs_ref[0, R_FOX:, :] = accfs_ref[R_FOX:, :]
        om_ref[0] = accm_ref[...] * (1.0 / lm_ref[...])

    @pl.when(last_step)
    def _():
        for s in range(nsub):
            for cp in sub_copies(nb, nc, 1 - slot, s):
                cp.wait()


def _sample_attn(layer, page_table, qops, new_ops, later, caches, pp):
    s_cnt, n_pages = page_table.shape
    assert n_pages % pp == 0 and pp % SUB_PAGES == 0
    nch = n_pages // pp

    def per_sample(a):
        return pl.BlockSpec((1,) + a.shape[1:], lambda b, c, pt: (b, 0, 0))

    in_specs = [per_sample(a) for a in qops] + [per_sample(a) for a in new_ops]
    in_specs.append(pl.BlockSpec(later.shape, lambda b, c, pt: (0, 0)))
    in_specs += [pl.BlockSpec(memory_space=pl.ANY)] * N_ARR
    bufs = [pltpu.VMEM((2, pp, FOX_HEADS, 8, PAGE) if k == 2 else (2, pp) + a.shape[2:], F32)
            for k, a in enumerate(caches)]
    return pl.pallas_call(
        functools.partial(_sample_kernel, pp, nch, layer),
        grid_spec=pltpu.PrefetchScalarGridSpec(
            num_scalar_prefetch=1,
            grid=(s_cnt, nch),
            in_specs=in_specs,
            out_specs=[pl.BlockSpec((1, R_FS, MXU), lambda b, c, pt: (b, 0, 0)),
                       pl.BlockSpec((1, R_MLA, LANE), lambda b, c, pt: (b, 0, 0))],
            scratch_shapes=[pltpu.VMEM((R_FOX, 1), F32), pltpu.VMEM((R_FOX, 1), F32),
                            pltpu.VMEM((R_MLA, 1), F32), pltpu.VMEM((R_MLA, 1), F32),
                            pltpu.VMEM((R_FS, MXU), F32), pltpu.VMEM((R_MLA, LANE), F32),
                            pltpu.VMEM((R_CUM, 1), F32)] + bufs
            + [pltpu.SemaphoreType.DMA((2, pp // SUB_PAGES, N_ARR))]),
        out_shape=[jax.ShapeDtypeStruct((s_cnt, R_FS, MXU), F32), jax.ShapeDtypeStruct((s_cnt, R_MLA, LANE), F32)],
        compiler_params=_cparams(("arbitrary", "arbitrary")),
        name="sample_attn",
    )(page_table, *qops, *new_ops, later, *caches)


def _sample_operands(r, s_cnt):
    def slots(q, nh, w, keep):
        return q.reshape(s_cnt, N_NEW, nh, w)[..., :keep].transpose(0, 2, 1, 3).reshape(s_cnt, nh * N_NEW, keep)

    def gqa(q):
        z = jnp.zeros_like(q)
        half = GROUP * N_NEW
        return jnp.concatenate([jnp.concatenate([q[:, :half], z[:, :half]], axis=2),
                                jnp.concatenate([z[:, half:], q[:, half:]], axis=2)], axis=1)

    qf = gqa(slots(r["qf"], FOX_HEADS, LANE, HEAD_DIM))
    qs = gqa(slots(r["qs"], SB_HEADS, LANE, HEAD_DIM))
    zf = jnp.zeros_like(qf)
    qfs = jnp.concatenate([jnp.concatenate([qf, zf], axis=2), jnp.concatenate([zf, qs], axis=2)], axis=1)
    qm = r["qm"].reshape(s_cnt, N_NEW, MLA_HEADS, MXU).transpose(0, 2, 1, 3).reshape(s_cnt, R_MLA, MXU)
    qm1, qm2 = qm[..., :MLA_KV_RANK], qm[..., MLA_KV_RANK:MLA_KV_RANK + MLA_ROPE]

    def pad_keys(a):
        a = a.reshape(s_cnt, N_NEW, a.shape[-1])
        return jnp.pad(a, ((0, 0), (0, PAGE - N_NEW), (0, 0)))

    nk = pad_keys(jnp.concatenate([r["fk32"], r["sk32"]], axis=1))
    nv = pad_keys(jnp.concatenate([r["fv32"], r["sv32"]], axis=1))
    nlf = r["logft"].reshape(8, s_cnt, N_NEW).transpose(1, 0, 2)
    nlf = jnp.where(jnp.arange(8)[None, :, None] < FOX_HEADS, nlf, 0.0)
    nlf = jnp.pad(nlf, ((0, 0), (0, 0), (0, PAGE - N_NEW)))
    return (qfs, qm1, qm2), (nk, nv, pad_keys(r["ckv32"]), pad_keys(r["kpe32"]), nlf)


def _sample_unpack(ofs, om, s_cnt):
    def gqa(a):
        a = a.reshape(s_cnt, FOX_KV_HEADS, GROUP, N_NEW, FOX_KV_HEADS, HEAD_DIM)
        a = jnp.stack([a[:, hk, :, :, hk, :] for hk in range(FOX_KV_HEADS)], axis=1)
        return a.transpose(0, 3, 1, 2, 4).reshape(s_cnt * N_NEW, FOX_W)

    o_fox = gqa(ofs[:, :R_FOX, :LANE])
    o_sb = gqa(ofs[:, R_FOX:, LANE:])
    o_lat = om.reshape(s_cnt, MLA_HEADS, N_NEW, MLA_KV_RANK).transpose(0, 2, 1, 3)
    return o_fox, o_lat.reshape(s_cnt * N_NEW, MLA_HEADS * MLA_KV_RANK), o_sb


def _uv_kernel(x_ref, w_ref, o_ref):
    for h in range(MLA_HEADS):
        x = x_ref[:, h * MLA_KV_RANK:(h + 1) * MLA_KV_RANK].astype(BF16)
        o_ref[:, h * MLA_V:(h + 1) * MLA_V] = _dot(x, w_ref[h])


def _uv_proj(o_lat, wuv):
    n = o_lat.shape[0]
    return pl.pallas_call(
        _uv_kernel,
        out_shape=jax.ShapeDtypeStruct((n, MLA_W), F32),
        name="uv_proj",
    )(o_lat, wuv)


def _out_kernel(of_ref, om_ref, os_ref, h_ref, gf_ref, gm_ref, gs_ref, w_ref, o_ref):
    a = _rms(of_ref[...], gf_ref[...]).astype(BF16)
    b = _rms(om_ref[...], gm_ref[...]).astype(BF16)
    c = _rms(os_ref[...], gs_ref[...]).astype(BF16)
    y = _dot(a, w_ref[0:FOX_W, :]) + _dot(b, w_ref[FOX_W:FOX_W + MLA_W, :]) + _dot(c, w_ref[FOX_W + MLA_W:, :])
    o_ref[...] = h_ref[...] + y


def _out_proj(o_fox, o_mla, o_sb, h, lw, tm):
    n = h.shape[0]

    def row(w):
        return pl.BlockSpec((tm, w), lambda i: (i, 0))

    def full(a):
        return pl.BlockSpec(a.shape, lambda i: (0, 0))

    return pl.pallas_call(
        _out_kernel,
        grid=(n // tm,),
        in_specs=[row(FOX_W), row(MLA_W), row(SB_W), row(D_MODEL), full(lw["g_of"]), full(lw["g_om"]),
                  full(lw["g_os"]), full(lw["w_out"])],
        out_specs=row(D_MODEL),
        out_shape=jax.ShapeDtypeStruct((n, D_MODEL), F32),
        compiler_params=_cparams(("arbitrary",)),
        name="out_proj",
    )(o_fox, o_mla, o_sb, h, lw["g_of"], lw["g_om"], lw["g_os"], lw["w_out"])


def _memkv_kernel(x_ref, w_ref, k_ref, v_ref):
    y = _dot(x_ref[...].astype(BF16), w_ref[0])
    k_ref[0] = y[:, :MEM_W]
    v_ref[0] = y[:, MEM_W:]


def _mem_kv(mem, w_kv, tm):
    n = mem.shape[0]
    nl = w_kv.shape[0]
    out = jax.ShapeDtypeStruct((nl, n, MEM_W), F32)
    ospec = pl.BlockSpec((1, tm, MEM_W), lambda i, l: (l, i, 0))
    return pl.pallas_call(
        _memkv_kernel,
        grid=(n // tm, nl),
        in_specs=[pl.BlockSpec((tm, D_MODEL), lambda i, l: (i, 0)),
                  pl.BlockSpec((1, D_MODEL, 2 * MEM_W), lambda i, l: (l, 0, 0))],
        out_specs=[ospec, ospec],
        out_shape=[out, out],
        compiler_params=_cparams(("arbitrary", "arbitrary")),
        name="mem_kv",
    )(mem, w_kv)


def _mem_kernel(ns, tq, h_ref, g_ref, wq_ref, k_ref, v_ref, wo_ref, o_ref):
    h = h_ref[...]
    hn = _rms(h, g_ref[...]).astype(BF16)
    q = _dot(hn, wq_ref[...]) * MEM_SCALE
    outs = []
    for hd in range(MEM_HEADS):
        sl = slice(hd * MEM_HEAD_DIM, (hd + 1) * MEM_HEAD_DIM)
        qh = q[:, sl].reshape(ns, tq, MEM_HEAD_DIM).astype(BF16)
        kh = k_ref[:, :, sl].astype(BF16)
        vh = v_ref[:, :, sl].astype(BF16)
        s = jnp.einsum("sqd,skd->sqk", qh, kh, preferred_element_type=F32)
        m = jnp.max(s, axis=-1, keepdims=True)
        p = jnp.exp(s - m)
        p = p / jnp.sum(p, axis=-1, keepdims=True)
        o = jnp.einsum("sqk,skd->sqd", p.astype(BF16), vh, preferred_element_type=F32)
        outs.append(o.reshape(ns * tq, MEM_HEAD_DIM))
    o = jnp.concatenate(outs, axis=1).astype(BF16)
    o_ref[...] = h + _dot(o, wo_ref[...])


def _mem_attn(h, mem_k, mem_v, lw, ns, tq, per_seq):
    n = h.shape[0]
    tm = ns * tq
    assert per_seq % tq == 0 and (ns == 1 or tq == per_seq)
    spb = per_seq // tq
    mt = mem_k.shape[1]

    def kv_index(i):
        return (i // spb if ns == 1 else i, 0, 0)

    def full(a):
        return pl.BlockSpec(a.shape, lambda i: (0, 0))

    return pl.pallas_call(
        functools.partial(_mem_kernel, ns, tq),
        grid=(n // tm,),
        in_specs=[pl.BlockSpec((tm, D_MODEL), lambda i: (i, 0)), full(lw["g_mem"]), full(lw["w_mq"]),
                  pl.BlockSpec((ns, mt, MEM_W), kv_index), pl.BlockSpec((ns, mt, MEM_W), kv_index),
                  full(lw["w_mo"])],
        out_specs=pl.BlockSpec((tm, D_MODEL), lambda i: (i, 0)),
        out_shape=jax.ShapeDtypeStruct((n, D_MODEL), F32),
        compiler_params=_cparams(("arbitrary",)),
        name="mem_attn",
    )(h, lw["g_mem"], lw["w_mq"], mem_k, mem_v, lw["w_mo"])


def _mlp_kernel(final, h_ref, g_ref, wu_ref, wd_ref, gfin_ref, o_ref, hn_ref, acc_ref):
    j = pl.program_id(1)

    @pl.when(j == 0)
    def _():
        hn_ref[...] = _rms(h_ref[...], g_ref[...]).astype(BF16)
        acc_ref[...] = h_ref[...]

    u = jnp.maximum(_dot(hn_ref[...], wu_ref[...]), 0.0)
    acc_ref[...] += _dot((u * u).astype(BF16), wd_ref[...])

    @pl.when(j == pl.num_programs(1) - 1)
    def _():
        y = acc_ref[...]
        o_ref[...] = _rms(y, gfin_ref[...]) if final else y


def _mlp(h, lw, g_final, final, tm, tf):
    n = h.shape[0]
    return pl.pallas_call(
        functools.partial(_mlp_kernel, final),
        grid=(n // tm, D_FF // tf),
        in_specs=[pl.BlockSpec((tm, D_MODEL), lambda i, j: (i, 0)),
                  pl.BlockSpec((1, D_MODEL), lambda i, j: (0, 0)),
                  pl.BlockSpec((D_MODEL, tf), lambda i, j: (0, j)),
                  pl.BlockSpec((tf, D_MODEL), lambda i, j: (j, 0)),
                  pl.BlockSpec((1, D_MODEL), lambda i, j: (0, 0))],
        out_specs=pl.BlockSpec((tm, D_MODEL), lambda i, j: (i, 0)),
        out_shape=jax.ShapeDtypeStruct((n, D_MODEL), F32),
        scratch_shapes=[pltpu.VMEM((tm, D_MODEL), BF16), pltpu.VMEM((tm, D_MODEL), F32)],
        compiler_params=_cparams(("arbitrary", "arbitrary")),
        name="mlp",
    )(h, lw["g_mlp"], lw["w_up"], lw["w_down"], g_final)


def _swap_halves(w, width):
    lead = w.shape[:-1]
    g = w.reshape(lead + (-1, 2, width // 2))
    return g[..., ::-1, :].reshape(w.shape)


def _lane_blocks(w, width):
    d = w.shape[0]
    g = w.reshape(d, -1, width)
    return jnp.pad(g, ((0, 0), (0, 0), (0, LANE - width))).reshape(d, -1)


def _prep_layer(l, w_in, b_f, g_mix, g_q, w_uq, g_kv, w_uk, w_uv, g_of, g_om, g_os, w_out,
                g_mem, w_mq, w_mo, g_mlp, w_up, w_down):
    sizes = (FOX_W, KV_W, KV_W, FOX_HEADS, MLA_Q_RANK, MLA_KV_RANK, MLA_ROPE, SB_W, KV_W, KV_W)
    pts, acc = [], 0
    for s in sizes[:-1]:
        acc += s
        pts.append(acc)
    fq, fk, fv, ff, mqa, mkv, mkpe, sq, sk, sv = jnp.split(w_in[l], pts, axis=1)
    d = w_in.shape[1]

    def z(w):
        return jnp.zeros((d, w), F32)

    w_in_r = jnp.concatenate(
        [fk, fv, sk, sv, mkv, mkpe, z(LANE - MLA_ROPE), _swap_halves(mkpe, MLA_ROPE), z(LANE - MLA_ROPE),
         ff, z(LANE - FOX_HEADS), mqa, z(2 * LANE - MLA_Q_RANK),
         _lane_blocks(fq, HEAD_DIM), _lane_blocks(fk, HEAD_DIM), _lane_blocks(fv, HEAD_DIM),
         _lane_blocks(sq, HEAD_DIM), _lane_blocks(sk, HEAD_DIM), _lane_blocks(sv, HEAD_DIM)], axis=1)
    assert w_in_r.shape[1] == N_INR
    uq = w_uq[l].reshape(MLA_Q_RANK, MLA_HEADS, MLA_NOPE + MLA_ROPE)
    nope = uq[:, :, :MLA_NOPE].reshape(MLA_Q_RANK, MLA_HEADS * MLA_NOPE)
    pe = uq[:, :, MLA_NOPE:].reshape(MLA_Q_RANK, MLA_HEADS * MLA_ROPE)
    w_uq_r = jnp.concatenate([nope, _lane_blocks(pe, MLA_ROPE), _lane_blocks(_swap_halves(pe, MLA_ROPE), MLA_ROPE)],
                             axis=1)
    assert w_uq_r.shape[1] == N_UQ
    w_uq_r = jnp.pad(w_uq_r, ((0, 2 * LANE - MLA_Q_RANK), (0, 0)))
    w_uk_r = jnp.pad(w_uk[l].transpose(1, 2, 0), ((0, 0), (0, 0), (0, 0)))
    return dict(
        g_mix=g_mix[l][None, :],
        w_in=w_in_r.astype(BF16),
        w_fft=jnp.pad(ff.T, ((0, 8 - FOX_HEADS), (0, 0))).astype(BF16),
        b_f=jnp.pad(b_f[l], (0, LANE - FOX_HEADS))[None, :],
        b_ft=jnp.pad(b_f[l], (0, 8 - FOX_HEADS))[:, None],
        g_q=jnp.pad(g_q[l], (0, 2 * LANE - MLA_Q_RANK))[None, :],
        w_uq=w_uq_r.astype(BF16),
        g_kv=g_kv[l][None, :],
        w_uk=w_uk_r.astype(BF16),
        w_uv=w_uv[l].transpose(1, 0, 2).astype(BF16),
        g_of=g_of[l][None, :], g_om=g_om[l][None, :], g_os=g_os[l][None, :],
        w_out=w_out[l].astype(BF16),
        g_mem=g_mem[l][None, :], w_mq=w_mq[l].astype(BF16), w_mo=w_mo[l].astype(BF16),
        g_mlp=g_mlp[l][None, :], w_up=w_up[l].astype(BF16), w_down=w_down[l].astype(BF16),
    )


def _constants():
    nblk = FOX_HEADS + 2 * FOX_KV_HEADS
    sel = np.zeros((3 * LANE, nblk * LANE), np.float32)
    ones = np.zeros((1, nblk * LANE), np.float32)
    for h in range(FOX_HEADS):
        hk, g = divmod(h, GROUP)
        for j in range(3):
            sel[j * LANE + h, h * LANE + X_C + j] = 1.0
            ones[0, h * LANE + X_G + 3 * g + j] = 1.0
            sel[j * LANE + h, (FOX_HEADS + hk) * LANE + X_G + 3 * g + j] = -1.0
    for hk in range(FOX_KV_HEADS):
        ones[0, (FOX_HEADS + hk) * LANE + X_C:(FOX_HEADS + hk) * LANE + X_C + 3] = 1.0
        ones[0, (FOX_HEADS + FOX_KV_HEADS + hk) * LANE + X_ONE_V] = 1.0
    idx = np.arange(MXU)
    later = (idx[:, None] > idx[None, :]).astype(np.float32)
    return dict(sel=jnp.asarray(sel, BF16), ones=jnp.asarray(ones, F32), later=jnp.asarray(later, BF16))


def _rope_tables(pos):
    half = MLA_ROPE // 2
    inv_freq = jnp.power(ROPE_BASE, -jnp.arange(half, dtype=F32) / half)
    ang = pos.astype(F32)[:, None] * inv_freq[None, :]
    cos, sin = jnp.cos(ang), jnp.sin(ang)
    pad = ((0, 0), (0, LANE - MLA_ROPE))
    return jnp.pad(jnp.concatenate([cos, cos], axis=1), pad), jnp.pad(jnp.concatenate([-sin, sin], axis=1), pad)


TM_IN = 512
TB_ATT = 512
TM_MLP = 1024
TF_MLP = 1024
TQ_MEM = 512
NS_MEM = 16
PP = 16


def kernel(x_prompt, x_sample, cache_fox_k, cache_fox_v, cache_fox_logf, cache_mla_ckv, cache_mla_kpe,
           cache_sb_k, cache_sb_v, cache_mem_k, cache_mem_v, page_table, mem_prompt,
           g_mix, w_in, b_fox_f, g_mla_q, w_mla_uq, g_mla_kv, w_mla_uk, w_mla_uv,
           g_out_fox, g_out_mla, g_out_sb, w_out, g_mem, w_mem_q, w_mem_k, w_mem_v, w_mem_o,
           g_mlp, w_up, w_down, g_final):
    bsz, seq, _ = x_prompt.shape
    s_cnt, n_new, _ = x_sample.shape
    depth = w_in.shape[0]
    past_len = page_table.shape[1] * PAGE
    assert n_new == N_NEW
    n_p, n_s = bsz * seq, s_cnt * n_new
    consts = _constants()

    cos_p, sin_p = _rope_tables(jnp.arange(seq, dtype=jnp.int32))
    cos_p, sin_p = jnp.tile(cos_p, (bsz, 1)), jnp.tile(sin_p, (bsz, 1))
    cos_s, sin_s = _rope_tables(past_len + jnp.arange(n_new, dtype=jnp.int32))
    cos_s, sin_s = jnp.tile(cos_s, (s_cnt, 1)), jnp.tile(sin_s, (s_cnt, 1))

    pool = cache_fox_k.shape[1]

    def keys_minor(c):
        return c.transpose(0, 1, 3, 4, 2).reshape(depth, pool, KV_W, PAGE)

    caches = (keys_minor(cache_fox_k), keys_minor(cache_fox_v), cache_fox_logf.transpose(0, 3, 1, 2),
              cache_mla_ckv, cache_mla_kpe.transpose(0, 1, 3, 2), keys_minor(cache_sb_k), keys_minor(cache_sb_v))
    mem_tok = mem_prompt.shape[1]
    w_kv = jnp.concatenate([w_mem_k, w_mem_v], axis=2).astype(BF16)
    p_mem_k, p_mem_v = _mem_kv(mem_prompt.reshape(bsz * mem_tok, D_MODEL), w_kv, 512)
    s_mem_k = cache_mem_k.reshape(depth, s_cnt, mem_tok, MEM_W)
    s_mem_v = cache_mem_v.reshape(depth, s_cnt, mem_tok, MEM_W)
    g_fin = g_final[None, :]

    hp = x_prompt.reshape(n_p, D_MODEL)
    hs = x_sample.reshape(n_s, D_MODEL)
    rows_p, rows_s = [], []
    tm_p, tm_s = min(TM_IN, seq), min(TM_IN, n_s)
    tmlp_p, tmlp_s = min(TM_MLP, n_p), min(TM_MLP, n_s)
    tb = min(TB_ATT, seq)
    tq_mem = min(TQ_MEM, seq)
    ns_mem = min(NS_MEM, s_cnt)
    pp = min(PP, page_table.shape[1])
    for l in range(depth):
        lw = _prep_layer(l, w_in, b_fox_f, g_mix, g_mla_q, w_mla_uq, g_mla_kv, w_mla_uk, w_mla_uv,
                         g_out_fox, g_out_mla, g_out_sb, w_out, g_mem, w_mem_q, w_mem_o, g_mlp, w_up, w_down)
        final = l == depth - 1

        r = _in_proj(hp, lw, consts, cos_p, sin_p, seq, tm_p)
        o_fox, o_mla, o_sb = _prompt_attn(bsz, seq, tb, r, consts["later"], lw["w_uv"])
        hp = _out_proj(o_fox, o_mla, o_sb, hp, lw, tm_p)
        hp = _mem_attn(hp, p_mem_k[l].reshape(bsz, mem_tok, MEM_W), p_mem_v[l].reshape(bsz, mem_tok, MEM_W),
                       lw, 1, tq_mem, seq)
        hp = _mlp(hp, lw, g_fin, final, tmlp_p, TF_MLP)
        rows_p.append(r)

        r = _in_proj(hs, lw, consts, cos_s, sin_s, n_new, tm_s)
        qops, new_ops = _sample_operands(r, s_cnt)
        ofs, om = _sample_attn(l, page_table, qops, new_ops, consts["later"], caches, pp)
        o_fox, o_lat, o_sb = _sample_unpack(ofs, om, s_cnt)
        o_mla = _uv_proj(o_lat, lw["w_uv"])
        hs = _out_proj(o_fox, o_mla, o_sb, hs, lw, tm_s)
        hs = _mem_attn(hs, s_mem_k[l], s_mem_v[l], lw, ns_mem, n_new, n_new)
        hs = _mlp(hs, lw, g_fin, final, tmlp_s, TF_MLP)
        rows_s.append(r)

    def stack(rows, key, shape):
        return jnp.stack([r[key] for r in rows]).reshape((depth,) + shape)

    def group_rows(rows, lead):
        return (stack(rows, "fk32", lead + (FOX_KV_HEADS, HEAD_DIM)), stack(rows, "fv32", lead + (FOX_KV_HEADS, HEAD_DIM)),
                stack(rows, "logf", lead + (FOX_HEADS,)), stack(rows, "ckv32", lead + (MLA_KV_RANK,)),
                stack(rows, "kpe32", lead + (MLA_ROPE,)),
                stack(rows, "sk32", lead + (SB_KV_HEADS, HEAD_DIM)), stack(rows, "sv32", lead + (SB_KV_HEADS, HEAD_DIM)))

    y_prompt = hp.reshape(bsz, seq, D_MODEL)
    y_sample = hs.reshape(s_cnt, n_new, D_MODEL)
    pm_shape = (depth, bsz, mem_tok, MEM_HEADS, MEM_HEAD_DIM)
    return ((y_prompt, y_sample) + group_rows(rows_p, (bsz, seq))
            + (p_mem_k.reshape(pm_shape), p_mem_v.reshape(pm_shape)) + group_rows(rows_s, (s_cnt, n_new)))
```
